```python
import jax, jax.numpy as jnp
from jax import lax
import numpy as np

D_MODEL = 2048
BATCH = 4
SEQ = 8192
DEPTH = 4
DEC_BATCH = 16
DEC_SEQ = 2048
PAST_LEN = 128

GRID_W = 64
ROPE_THETA = 10000.0
Q_BLOCK = 128
EPS = 1e-6

A_HEADS = 8
A_KV_HEADS = 2
A_HEAD_DIM = 128
A_GROUP = A_HEADS // A_KV_HEADS
A_WIDTH = A_HEADS * A_HEAD_DIM

B_HEADS = 8
B_NOPE = 128
B_ROPE = 64
B_QK = B_NOPE + B_ROPE
B_V = 128
B_KV_RANK = 512
B_WIDTH = B_HEADS * B_V

N_BRANCH = 2

SPLIT_SIZES = (
    A_HEADS * A_HEAD_DIM,
    A_KV_HEADS * A_HEAD_DIM,
    A_KV_HEADS * A_HEAD_DIM,
    A_WIDTH,
    B_HEADS * B_QK,
    B_KV_RANK,
    B_ROPE,
    B_WIDTH,
    D_MODEL,
    D_MODEL,
)
IN_WIDTH = sum(SPLIT_SIZES)
SPLIT_POINTS = tuple(int(v) for v in np.cumsum(SPLIT_SIZES)[:-1])

kernel_name = "hybrid_gqa_mla_parallel_encoder"


def rmsnorm(x, g):
    xf = x.astype(jnp.float32)
    y = xf * lax.rsqrt(jnp.mean(xf * xf, axis=-1, keepdims=True) + EPS)
    return (y * g.astype(jnp.float32)).astype(x.dtype)


def axial_rope_tables(n_tokens, rot_dim):
    rows = n_tokens // GRID_W
    row = jnp.repeat(jnp.arange(rows, dtype=jnp.float32), GRID_W)
    col = (jnp.arange(rows * GRID_W) % GRID_W).astype(jnp.float32)
    n_freq = rot_dim // 4
    inv = ROPE_THETA ** (-jnp.arange(n_freq, dtype=jnp.float32) / n_freq)
    ang = jnp.concatenate([row[:, None] * inv, col[:, None] * inv], axis=-1)
    return jnp.cos(ang), jnp.sin(ang)


def apply_rope(x, cos, sin):
    half = x.shape[-1] // 2
    xf = x.astype(jnp.float32)
    x1, x2 = xf[..., :half], xf[..., half:]
    c = cos[None, :, None, :]
    s = sin[None, :, None, :]
    out = jnp.concatenate([x1 * c - x2 * s, x1 * s + x2 * c], axis=-1)
    return out.astype(x.dtype)


def block_attention(q, k, v, scale):
    b, s, hk, g, d = q.shape
    nb = s // Q_BLOCK
    qb = q.reshape(b, nb, Q_BLOCK, hk, g, d).transpose(1, 0, 2, 3, 4, 5)

    def attend(q_blk):
        sc = jnp.einsum('bqhgd,bkhd->bhgqk', q_blk, k,
                        preferred_element_type=jnp.float32) * scale
        p = jax.nn.softmax(sc, axis=-1).astype(v.dtype)
        return jnp.einsum('bhgqk,bkhe->bqhge', p, v)

    o = lax.map(attend, qb)
    o = o.transpose(1, 0, 2, 3, 4, 5)
    return o.reshape(b, s, hk * g * v.shape[-1])


def mixer_layer(x, rope_a, rope_b, norm_g, w_in, q_norm_g, k_norm_g,
                kv_norm_g, w_ukv, w_branch, w_out):
    b, s, _ = x.shape
    h = rmsnorm(x, norm_g)
    z = jnp.einsum('bsd,de->bse', h, w_in)
    (qa, ka, va, gate_a, qb, ckv, krope, gate_b,
     mix_a, mix_b) = jnp.split(z, SPLIT_POINTS, axis=-1)

    cos_a, sin_a = rope_a
    qa = rmsnorm(qa.reshape(b, s, A_HEADS, A_HEAD_DIM), q_norm_g)
    ka = rmsnorm(ka.reshape(b, s, A_KV_HEADS, A_HEAD_DIM), k_norm_g)
    va = va.reshape(b, s, A_KV_HEADS, A_HEAD_DIM)
    qa = apply_rope(qa, cos_a, sin_a).reshape(b, s, A_KV_HEADS, A_GROUP, A_HEAD_DIM)
    ka = apply_rope(ka, cos_a, sin_a)
    oa = block_attention(qa, ka, va, A_HEAD_DIM ** -0.5)
    oa = oa * jax.nn.silu(gate_a)
    pa = jnp.einsum('bse,ed->bsd', oa, w_branch[0])

    cos_b, sin_b = rope_b
    qb = qb.reshape(b, s, B_HEADS, B_QK)
    q_nope, q_rot = qb[..., :B_NOPE], qb[..., B_NOPE:]
    q_rot = apply_rope(q_rot, cos_b, sin_b)
    c = rmsnorm(ckv, kv_norm_g)
    kv = jnp.einsum('bsr,re->bse', c, w_ukv).reshape(b, s, B_HEADS, B_NOPE + B_V)
    k_nope, vb = kv[..., :B_NOPE], kv[..., B_NOPE:]
    k_rot = apply_rope(krope.reshape(b, s, 1, B_ROPE), cos_b, sin_b)
    k_rot = jnp.broadcast_to(k_rot, (b, s, B_HEADS, B_ROPE))
    q_full = jnp.concatenate([q_nope, q_rot], axis=-1)[:, :, :, None, :]
    k_full = jnp.concatenate([k_nope, k_rot], axis=-1)
    ob = block_attention(q_full, k_full, vb, B_QK ** -0.5)
    ob = ob * jax.nn.silu(gate_b)
    pb = jnp.einsum('bse,ed->bsd', ob, w_branch[1])

    merged = jax.nn.sigmoid(mix_a) * pa + jax.nn.sigmoid(mix_b) * pb
    return x + jnp.einsum('bsd,de->bse', merged, w_out)


def setup_inputs(seed: int = 0) -> dict:
    key = jax.random.key(seed)
    ks = jax.random.split(key, 11)
    f32 = jnp.float32
    x_prompt = jax.random.normal(ks[0], (BATCH, SEQ, D_MODEL), f32)
    x_sample = jax.random.normal(ks[1], (DEC_BATCH, DEC_SEQ, D_MODEL), f32)
    norm_g = 1.0 + 0.02 * jax.random.normal(ks[2], (DEPTH, D_MODEL), f32)
    w_in = jax.random.normal(ks[3], (DEPTH, D_MODEL, IN_WIDTH), f32) * D_MODEL ** -0.5
    q_norm_g = 1.0 + 0.02 * jax.random.normal(ks[4], (DEPTH, A_HEAD_DIM), f32)
    k_norm_g = 1.0 + 0.02 * jax.random.normal(ks[5], (DEPTH, A_HEAD_DIM), f32)
    kv_norm_g = 1.0 + 0.02 * jax.random.normal(ks[6], (DEPTH, B_KV_RANK), f32)
    w_ukv = jax.random.normal(ks[7], (DEPTH, B_KV_RANK, B_HEADS * (B_NOPE + B_V)), f32) * B_KV_RANK ** -0.5
    w_branch = jax.random.normal(ks[8], (DEPTH, N_BRANCH, A_WIDTH, D_MODEL), f32) * A_WIDTH ** -0.5
    w_out = jax.random.normal(ks[9], (DEPTH, D_MODEL, D_MODEL), f32) * D_MODEL ** -0.5
    final_norm_g = 1.0 + 0.02 * jax.random.normal(ks[10], (D_MODEL,), f32)
    return {"x_prompt": x_prompt, "x_sample": x_sample, "norm_g": norm_g,
            "w_in": w_in, "q_norm_g": q_norm_g, "k_norm_g": k_norm_g,
            "kv_norm_g": kv_norm_g, "w_ukv": w_ukv, "w_branch": w_branch,
            "w_out": w_out, "final_norm_g": final_norm_g}


def reference(x_prompt, x_sample, norm_g, w_in, q_norm_g, k_norm_g, kv_norm_g,
              w_ukv, w_branch, w_out, final_norm_g):
    sp = x_prompt.shape[1]
    ss = x_sample.shape[1]
    rope_a_p = axial_rope_tables(sp, A_HEAD_DIM)
    rope_b_p = axial_rope_tables(sp, B_ROPE)
    rope_a_s = axial_rope_tables(ss, A_HEAD_DIM)
    rope_b_s = axial_rope_tables(ss, B_ROPE)
    xp, xs = x_prompt, x_sample
    for l in range(DEPTH):
        params = (norm_g[l], w_in[l], q_norm_g[l], k_norm_g[l], kv_norm_g[l],
                  w_ukv[l], w_branch[l], w_out[l])
        xp = mixer_layer(xp, rope_a_p, rope_b_p, *params)
        xs = mixer_layer(xs, rope_a_s, rope_b_s, *params)
    y_prompt = rmsnorm(xp, final_norm_g)
    y_sample = rmsnorm(xs, final_norm_g)
    return (y_prompt, y_sample)
```

```python
import functools

import jax
import jax.numpy as jnp
from jax import lax
from jax.experimental import pallas as pl
from jax.experimental.pallas import tpu as pltpu

F32 = jnp.float32
BF16 = jnp.bfloat16

GRID_W = 64
ROPE_THETA = 10000.0
EPS = 1e-6
A_HEADS = 8
A_KV_HEADS = 2
A_HEAD_DIM = 128
A_GROUP = A_HEADS // A_KV_HEADS
A_WIDTH = A_HEADS * A_HEAD_DIM
B_HEADS = 8
B_NOPE = 128
B_ROPE = 64
B_QK = B_NOPE + B_ROPE
B_QK_PAD = 256
B_V = 128
B_KV_RANK = 512
B_WIDTH = B_HEADS * B_V
LANES = 128

VMEM_LIMIT = 56 * 1024 * 1024


def _params(**kw):
    return pltpu.CompilerParams(vmem_limit_bytes=VMEM_LIMIT, **kw)


def _resident(shape):
    nd = len(shape)
    return pl.BlockSpec(shape, lambda *_: (0,) * nd, pipeline_mode=pl.Buffered(1))


def _rms(z, g):
    return z * lax.rsqrt(jnp.mean(z * z, axis=-1, keepdims=True) + EPS) * g


def _norm_kernel(x_ref, g_ref, h_ref):
    h_ref[...] = _rms(x_ref[...], g_ref[...]).astype(BF16)


def _input_norm(x, g, tm):
    t, d = x.shape
    return pl.pallas_call(
        _norm_kernel,
        grid=(t // tm,),
        in_specs=[pl.BlockSpec((tm, d), lambda i: (i, 0)),
                  pl.BlockSpec((1, d), lambda i: (0, 0))],
        out_specs=pl.BlockSpec((tm, d), lambda i: (i, 0)),
        out_shape=jax.ShapeDtypeStruct((t, d), BF16),
        name="input_norm",
        compiler_params=_params(),
    )(x, g)


def _proj_a_kernel(h_ref, w_ref, qg_ref, kg_ref, cq_ref, sq_ref, ck_ref, sk_ref,
                   q_ref, k_ref, vt_ref):
    z = jnp.dot(h_ref[...], w_ref[...], preferred_element_type=F32)
    d = A_HEAD_DIM
    for hd in range(A_HEADS):
        y = _rms(z[:, hd * d:(hd + 1) * d], qg_ref[...])
        y = y * cq_ref[...] + pltpu.roll(y, d // 2, 1) * sq_ref[...]
        q_ref[:, hd * d:(hd + 1) * d] = y.astype(BF16)
    k0 = A_WIDTH
    v0 = A_WIDTH + A_KV_HEADS * d
    for hd in range(A_KV_HEADS):
        y = _rms(z[:, k0 + hd * d:k0 + (hd + 1) * d], kg_ref[...])
        y = y * ck_ref[...] + pltpu.roll(y, d // 2, 1) * sk_ref[...]
        k_ref[:, hd * d:(hd + 1) * d] = y.astype(BF16)
        vt_ref[hd, 0] = z[:, v0 + hd * d:v0 + (hd + 1) * d].T.astype(BF16)


def _proj_a(h, w_a, qg, kg, tabs, seq, tm):
    t, dm = h.shape
    nw = w_a.shape[1]
    npos = seq // tm
    tab = pl.BlockSpec((tm, LANES), lambda i: (i % npos, 0))
    return pl.pallas_call(
        _proj_a_kernel,
        grid=(t // tm,),
        in_specs=[pl.BlockSpec((tm, dm), lambda i: (i, 0)),
                  _resident((dm, nw)),
                  _resident((1, LANES)), _resident((1, LANES)),
                  tab, tab, tab, tab],
        out_specs=[pl.BlockSpec((tm, A_WIDTH), lambda i: (i, 0)),
                   pl.BlockSpec((tm, A_KV_HEADS * A_HEAD_DIM), lambda i: (i, 0)),
                   pl.BlockSpec((A_KV_HEADS, 1, A_HEAD_DIM, tm), lambda i: (0, i, 0, 0))],
        out_shape=[jax.ShapeDtypeStruct((t, A_WIDTH), BF16),
                   jax.ShapeDtypeStruct((t, A_KV_HEADS * A_HEAD_DIM), BF16),
                   jax.ShapeDtypeStruct((A_KV_HEADS, t // tm, A_HEAD_DIM, tm), BF16)],
        name="proj_a",
        compiler_params=_params(),
    )(h, w_a, qg, kg, *tabs)


def _rope_b(y, c_ref, s1_ref, s2_ref):
    return (y * c_ref[...] + pltpu.roll(y, LANES - B_ROPE // 2, 1) * s1_ref[...]
            + pltpu.roll(y, B_ROPE // 2, 1) * s2_ref[...])


def _proj_b_kernel(h_ref, w_ref, kvg_ref, wukv_ref, cq_ref, s1q_ref, s2q_ref,
                   ck_ref, s1k_ref, s2k_ref, q_ref, k_ref, vt_ref):
    z = jnp.dot(h_ref[...], w_ref[...], preferred_element_type=F32)
    scale = B_QK ** -0.5
    p = B_QK_PAD
    for hd in range(B_HEADS):
        q_ref[:, hd * p:hd * p + B_NOPE] = (z[:, hd * p:hd * p + B_NOPE] * scale).astype(BF16)
        qr = _rope_b(z[:, hd * p + B_NOPE:(hd + 1) * p], cq_ref, s1q_ref, s2q_ref)
        q_ref[:, hd * p + B_NOPE:(hd + 1) * p] = qr.astype(BF16)
    c0 = B_HEADS * p
    c = _rms(z[:, c0:c0 + B_KV_RANK], kvg_ref[...]).astype(BF16)
    kr = _rope_b(z[:, c0 + B_KV_RANK:c0 + B_KV_RANK + LANES], ck_ref, s1k_ref, s2k_ref).astype(BF16)
    kv = jnp.dot(c, wukv_ref[...], preferred_element_type=F32)
    for hd in range(B_HEADS):
        k_ref[:, hd * p:hd * p + B_NOPE] = kv[:, hd * p:hd * p + B_NOPE].astype(BF16)
        k_ref[:, hd * p + B_NOPE:(hd + 1) * p] = kr
        vt_ref[hd, 0] = kv[:, hd * p + B_NOPE:(hd + 1) * p].T.astype(BF16)


def _proj_b(h, w_b, kvg, w_ukv, tabs, seq, tm):
    t, dm = h.shape
    nw = w_b.shape[1]
    npos = seq // tm
    tab = pl.BlockSpec((tm, LANES), lambda i: (i % npos, 0))
    wide = B_HEADS * B_QK_PAD
    return pl.pallas_call(
        _proj_b_kernel,
        grid=(t // tm,),
        in_specs=[pl.BlockSpec((tm, dm), lambda i: (i, 0)),
                  _resident((dm, nw)),
                  _resident((1, B_KV_RANK)),
                  _resident(w_ukv.shape),
                  tab, tab, tab, tab, tab, tab],
        out_specs=[pl.BlockSpec((tm, wide), lambda i: (i, 0)),
                   pl.BlockSpec((tm, wide), lambda i: (i, 0)),
                   pl.BlockSpec((B_HEADS, 1, B_V, tm), lambda i: (0, i, 0, 0))],
        out_shape=[jax.ShapeDtypeStruct((t, wide), BF16),
                   jax.ShapeDtypeStruct((t, wide), BF16),
                   jax.ShapeDtypeStruct((B_HEADS, t // tm, B_V, tm), BF16)],
        name="proj_b",
        compiler_params=_params(),
    )(h, w_b, kvg, w_ukv, *tabs)


def _gates_kernel(h_ref, w_ref, o_ref):
    z = jnp.dot(h_ref[...], w_ref[...], preferred_element_type=F32)
    sig = 1.0 / (1.0 + jnp.exp(-z))
    lin = jnp.where(pl.program_id(0) == 0, z, jnp.ones_like(z))
    o_ref[...] = (sig * lin).astype(BF16)


def _gates(h, w_g, tm, tn):
    t, dm = h.shape
    n = w_g.shape[1]
    return pl.pallas_call(
        _gates_kernel,
        grid=(n // tn, t // tm),
        in_specs=[pl.BlockSpec((tm, dm), lambda j, i: (i, 0)),
                  pl.BlockSpec((dm, tn), lambda j, i: (0, j))],
        out_specs=pl.BlockSpec((tm, tn), lambda j, i: (i, j)),
        out_shape=jax.ShapeDtypeStruct((t, n), BF16),
        name="gates",
        compiler_params=_params(),
    )(h, w_g)


def _attn_kernel(q_ref, k_ref, vt_ref, g_ref, o_ref, m_ref, l_ref, acc_ref,
                 *, group, dk, dv, tq, tk, n_chunks):
    if group == 1:
        q = q_ref[...]
    else:
        q = jnp.concatenate([q_ref[:, g * dk:(g + 1) * dk] for g in range(group)], axis=0)
    m_ref[...] = jnp.full(m_ref.shape, -jnp.inf, F32)
    l_ref[...] = jnp.zeros(l_ref.shape, F32)
    acc_ref[...] = jnp.zeros(acc_ref.shape, F32)

    def chunk(j, carry):
        kc = k_ref[pl.ds(pl.multiple_of(j * tk, tk), tk), :]
        s = lax.dot_general(kc, q, (((1,), (1,)), ((), ())),
                            preferred_element_type=F32)
        m_old = m_ref[...]
        m_new = jnp.maximum(m_old, jnp.max(s, axis=0, keepdims=True))
        alpha = jnp.exp(m_old - m_new)
        p = jnp.exp(s - m_new)
        l_ref[...] = alpha * l_ref[...] + jnp.sum(p, axis=0, keepdims=True)
        acc_ref[...] = alpha * acc_ref[...] + jnp.dot(
            vt_ref[j], p.astype(BF16), preferred_element_type=F32)
        m_ref[...] = m_new
        return carry

    lax.fori_loop(0, n_chunks, chunk, 0)
    o = acc_ref[...] * (1.0 / l_ref[...])
    for g in range(group):
        og = o[:, g * tq:(g + 1) * tq].T
        gate = g_ref[:, g * dv:(g + 1) * dv].astype(F32)
        o_ref[:, g * dv:(g + 1) * dv] = (og * gate).astype(BF16)


def _attention(q, k, vt, gates, *, batch, seq, kv_heads, group, dk, dv, tq, tk, gate_col0):
    n_chunks = seq // tk
    cols = group * tq
    q3 = q.reshape(batch, seq, q.shape[1])
    k3 = k.reshape(batch, seq, k.shape[1])
    g3 = gates.reshape(batch, seq, gates.shape[1])
    gw = group * dv
    gate_blk0 = gate_col0 // gw
    kern = functools.partial(_attn_kernel, group=group, dk=dk, dv=dv, tq=tq, tk=tk,
                             n_chunks=n_chunks)
    out = pl.pallas_call(
        kern,
        grid=(batch, kv_heads, seq // tq),
        in_specs=[pl.BlockSpec((None, tq, group * dk), lambda b, h, i: (b, i, h)),
                  pl.BlockSpec((None, seq, dk), lambda b, h, i: (b, 0, h)),
                  pl.BlockSpec((None, n_chunks, dv, tk), lambda b, h, i: (h, b, 0, 0)),
                  pl.BlockSpec((None, tq, gw), lambda b, h, i: (b, i, gate_blk0 + h))],
        out_specs=pl.BlockSpec((None, tq, gw), lambda b, h, i: (b, i, h)),
        out_shape=jax.ShapeDtypeStruct((batch, seq, kv_heads * gw), BF16),
        scratch_shapes=[pltpu.VMEM((1, cols), F32), pltpu.VMEM((1, cols), F32),
                        pltpu.VMEM((dv, cols), F32)],
        name="attn_gqa" if group > 1 else "attn_mla",
        compiler_params=_params(),
    )(q3, k3, vt, g3)
    return out.reshape(batch * seq, kv_heads * gw)


def _out_kernel(oa_ref, ob_ref, ma_ref, mb_ref, x_ref, wa_ref, wb_ref, wo_ref, g_ref,
                *out_refs, last):
    pa = jnp.dot(oa_ref[...], wa_ref[...], preferred_element_type=F32)
    pb = jnp.dot(ob_ref[...], wb_ref[...], preferred_element_type=F32)
    merged = ma_ref[...].astype(F32) * pa + mb_ref[...].astype(F32) * pb
    y = x_ref[...] + jnp.dot(merged.astype(BF16), wo_ref[...], preferred_element_type=F32)
    if last:
        out_refs[0][...] = _rms(y, g_ref[...])
    else:
        out_refs[0][...] = y
        out_refs[1][...] = _rms(y, g_ref[...]).astype(BF16)


def _merge_out(oa, ob, gates, x, w_br_a, w_br_b, w_o, g_next, tm, last):
    t, dm = x.shape
    wa = oa.shape[1]
    mix0 = (A_WIDTH + B_WIDTH) // dm
    row = lambda i: (i, 0)
    out_specs = [pl.BlockSpec((tm, dm), row)]
    out_shape = [jax.ShapeDtypeStruct((t, dm), F32)]
    if not last:
        out_specs.append(pl.BlockSpec((tm, dm), row))
        out_shape.append(jax.ShapeDtypeStruct((t, dm), BF16))
    return pl.pallas_call(
        functools.partial(_out_kernel, last=last),
        grid=(t // tm,),
        in_specs=[pl.BlockSpec((tm, wa), row),
                  pl.BlockSpec((tm, wa), row),
                  pl.BlockSpec((tm, dm), lambda i: (i, mix0)),
                  pl.BlockSpec((tm, dm), lambda i: (i, mix0 + 1)),
                  pl.BlockSpec((tm, dm), row),
                  _resident(w_br_a.shape), _resident(w_br_b.shape), _resident(w_o.shape),
                  _resident((1, dm))],
        out_specs=out_specs,
        out_shape=out_shape,
        name="merge_out_final" if last else "merge_out",
        compiler_params=_params(),
    )(oa, ob, gates, gates, x, w_br_a, w_br_b, w_o, g_next)


def _axial_tables(n_tokens, rot_dim):
    rows = n_tokens // GRID_W
    row = jnp.repeat(jnp.arange(rows, dtype=F32), GRID_W)
    col = (jnp.arange(rows * GRID_W) % GRID_W).astype(F32)
    n_freq = rot_dim // 4
    inv = ROPE_THETA ** (-jnp.arange(n_freq, dtype=F32) / n_freq)
    ang = jnp.concatenate([row[:, None] * inv, col[:, None] * inv], axis=-1)
    return jnp.cos(ang), jnp.sin(ang)


def _rope_tables(seq):
    ca, sa = _axial_tables(seq, A_HEAD_DIM)
    c_a = jnp.concatenate([ca, ca], axis=-1)
    s_a = jnp.concatenate([-sa, sa], axis=-1)
    sc_a = A_HEAD_DIM ** -0.5
    tabs_a = (c_a * sc_a, s_a * sc_a, c_a, s_a)
    cb, sb = _axial_tables(seq, B_ROPE)
    z32 = jnp.zeros_like(cb)
    c_b = jnp.concatenate([cb, cb, z32, z32], axis=-1)
    s1_b = jnp.concatenate([-sb, z32, z32, z32], axis=-1)
    s2_b = jnp.concatenate([z32, sb, z32, z32], axis=-1)
    sc_b = B_QK ** -0.5
    tabs_b = (c_b * sc_b, s1_b * sc_b, s2_b * sc_b, c_b, s1_b, s2_b)
    return tabs_a, tabs_b


def _split_w_in(w_in):
    depth, dm, _ = w_in.shape
    kvw = A_KV_HEADS * A_HEAD_DIM
    sizes = (A_WIDTH, kvw, kvw, A_WIDTH, B_HEADS * B_QK, B_KV_RANK, B_ROPE, B_WIDTH, dm, dm)
    parts = []
    off = 0
    for s in sizes:
        parts.append(w_in[:, :, off:off + s])
        off += s
    qa, ka, va, gate_a, qb, ckv, krope, gate_b, mix_a, mix_b = parts
    w_a = jnp.concatenate([qa, ka, va], axis=-1).astype(BF16)
    qb = qb.reshape(depth, dm, B_HEADS, B_QK)
    qb = jnp.pad(qb, ((0, 0), (0, 0), (0, 0), (0, B_QK_PAD - B_QK)))
    qb = qb.reshape(depth, dm, B_HEADS * B_QK_PAD)
    krope = jnp.pad(krope, ((0, 0), (0, 0), (0, LANES - B_ROPE)))
    w_b = jnp.concatenate([qb, ckv, krope], axis=-1).astype(BF16)
    w_g = jnp.concatenate([gate_a, gate_b, mix_a, mix_b], axis=-1).astype(BF16)
    return w_a, w_b, w_g


def _tiles(seq):
    tm = min(512, seq)
    tq_a = min(128, seq)
    tq_b = min(512, seq)
    return tm, tq_a, tq_b


def _trunk(x, weights, final_g):
    batch, seq, dm = x.shape
    (norm_g, w_a, w_b, w_g, q_norm_g, k_norm_g, kv_norm_g, w_ukv, w_branch, w_out) = weights
    depth = norm_g.shape[0]
    tm, tq_a, tq_b = _tiles(seq)
    tabs_a, tabs_b = _rope_tables(seq)
    xt = x.reshape(batch * seq, dm)
    h = _input_norm(xt, norm_g[0][None, :], tm)
    for l in range(depth):
        q_a, k_a, vt_a = _proj_a(h, w_a[l], q_norm_g[l][None, :], k_norm_g[l][None, :],
                                 tabs_a, seq, tm)
        q_b, k_b, vt_b = _proj_b(h, w_b[l], kv_norm_g[l][None, :], w_ukv[l], tabs_b, seq, tm)
        gates = _gates(h, w_g[l], tm, dm)
        oa = _attention(q_a, k_a, vt_a, gates, batch=batch, seq=seq, kv_heads=A_KV_HEADS,
                        group=A_GROUP, dk=A_HEAD_DIM, dv=A_HEAD_DIM, tq=tq_a, tk=tm,
                        gate_col0=0)
        ob = _attention(q_b, k_b, vt_b, gates, batch=batch, seq=seq, kv_heads=B_HEADS,
                        group=1, dk=B_QK_PAD, dv=B_V, tq=tq_b, tk=tm, gate_col0=A_WIDTH)
        last = l == depth - 1
        g_next = final_g if last else norm_g[l + 1]
        res = _merge_out(oa, ob, gates, xt, w_branch[l, 0], w_branch[l, 1], w_out[l],
                         g_next[None, :], min(256, tm), last)
        if last:
            xt = res[0]
        else:
            xt, h = res
    return xt.reshape(batch, seq, dm)


def kernel(x_prompt, x_sample, norm_g, w_in, q_norm_g, k_norm_g, kv_norm_g, w_ukv, w_branch,
           w_out, final_norm_g):
    w_a, w_b, w_g = _split_w_in(w_in)
    weights = (norm_g, w_a, w_b, w_g, q_norm_g, k_norm_g, kv_norm_g, w_ukv.astype(BF16),
               w_branch.astype(BF16), w_out.astype(BF16))
    y_prompt = _trunk(x_prompt, weights, final_norm_g)
    y_sample = _trunk(x_sample, weights, final_norm_g)
    return (y_prompt, y_sample)
```

```python
import functools

import jax
import jax.numpy as jnp
from jax import lax
from jax.experimental import pallas as pl
from jax.experimental.pallas import tpu as pltpu

F32 = jnp.float32
BF16 = jnp.bfloat16

GRID_W = 64
ROPE_THETA = 10000.0
EPS = 1e-6
A_HEADS = 8
A_KV_HEADS = 2
A_HEAD_DIM = 128
A_GROUP = A_HEADS // A_KV_HEADS
A_WIDTH = A_HEADS * A_HEAD_DIM
B_HEADS = 8
B_NOPE = 128
B_ROPE = 64
B_QK = B_NOPE + B_ROPE
B_QK_PAD = 256
B_V = 128
B_KV_RANK = 512
B_WIDTH = B_HEADS * B_V
LANES = 128
LOG2_E = 1.4426950408889634

VMEM_LIMIT = 56 * 1024 * 1024


def _params(**kw):
    return pltpu.CompilerParams(vmem_limit_bytes=VMEM_LIMIT, **kw)


def _resident(shape):
    nd = len(shape)
    return pl.BlockSpec(shape, lambda *_: (0,) * nd, pipeline_mode=pl.Buffered(1))


def _rms(z, g):
    return z * lax.rsqrt(jnp.mean(z * z, axis=-1, keepdims=True) + EPS) * g


def _norm_kernel(x_ref, g_ref, h_ref):
    h_ref[...] = _rms(x_ref[...], g_ref[...]).astype(BF16)


def _input_norm(x, g, tm):
    t, d = x.shape
    return pl.pallas_call(
        _norm_kernel,
        grid=(t // tm,),
        in_specs=[pl.BlockSpec((tm, d), lambda i: (i, 0)),
                  pl.BlockSpec((1, d), lambda i: (0, 0))],
        out_specs=pl.BlockSpec((tm, d), lambda i: (i, 0)),
        out_shape=jax.ShapeDtypeStruct((t, d), BF16),
        name="input_norm",
        compiler_params=_params(),
    )(x, g)


def _proj_a_kernel(h_ref, w_ref, qg_ref, kg_ref, cq_ref, sq_ref, ck_ref, sk_ref,
                   q_ref, k_ref, vt_ref):
    z = jnp.dot(h_ref[...], w_ref[...], preferred_element_type=F32)
    d = A_HEAD_DIM
    for hd in range(A_HEADS):
        y = _rms(z[:, hd * d:(hd + 1) * d], qg_ref[...])
        y = y * cq_ref[...] + pltpu.roll(y, d // 2, 1) * sq_ref[...]
        q_ref[:, hd * d:(hd + 1) * d] = y.astype(BF16)
    k0 = A_WIDTH
    v0 = A_WIDTH + A_KV_HEADS * d
    for hd in range(A_KV_HEADS):
        y = _rms(z[:, k0 + hd * d:k0 + (hd + 1) * d], kg_ref[...])
        y = y * ck_ref[...] + pltpu.roll(y, d // 2, 1) * sk_ref[...]
        k_ref[:, hd * d:(hd + 1) * d] = y.astype(BF16)
        vt_ref[hd, 0] = z[:, v0 + hd * d:v0 + (hd + 1) * d].T.astype(BF16)


def _proj_a(h, w_a, qg, kg, tabs, seq, tm):
    t, dm = h.shape
    nw = w_a.shape[1]
    npos = seq // tm
    tab = pl.BlockSpec((tm, LANES), lambda i: (i % npos, 0))
    return pl.pallas_call(
        _proj_a_kernel,
        grid=(t // tm,),
        in_specs=[pl.BlockSpec((tm, dm), lambda i: (i, 0)),
                  _resident((dm, nw)),
                  _resident((1, LANES)), _resident((1, LANES)),
                  tab, tab, tab, tab],
        out_specs=[pl.BlockSpec((tm, A_WIDTH), lambda i: (i, 0)),
                   pl.BlockSpec((tm, A_KV_HEADS * A_HEAD_DIM), lambda i: (i, 0)),
                   pl.BlockSpec((A_KV_HEADS, 1, A_HEAD_DIM, tm), lambda i: (0, i, 0, 0))],
        out_shape=[jax.ShapeDtypeStruct((t, A_WIDTH), BF16),
                   jax.ShapeDtypeStruct((t, A_KV_HEADS * A_HEAD_DIM), BF16),
                   jax.ShapeDtypeStruct((A_KV_HEADS, t // tm, A_HEAD_DIM, tm), BF16)],
        name="proj_a",
        compiler_params=_params(),
    )(h, w_a, qg, kg, *tabs)


def _rope_b(y, c_ref, s1_ref, s2_ref):
    return (y * c_ref[...] + pltpu.roll(y, LANES - B_ROPE // 2, 1) * s1_ref[...]
            + pltpu.roll(y, B_ROPE // 2, 1) * s2_ref[...])


def _proj_b_kernel(h_ref, w_ref, kvg_ref, wukv_ref, cq_ref, s1q_ref, s2q_ref,
                   ck_ref, s1k_ref, s2k_ref, q_ref, k_ref, vt_ref):
    z = jnp.dot(h_ref[...], w_ref[...], preferred_element_type=F32)
    scale = B_QK ** -0.5 * LOG2_E
    p = B_QK_PAD
    for hd in range(B_HEADS):
        q_ref[:, hd * p:hd * p + B_NOPE] = (z[:, hd * p:hd * p + B_NOPE] * scale).astype(BF16)
        qr = _rope_b(z[:, hd * p + B_NOPE:(hd + 1) * p], cq_ref, s1q_ref, s2q_ref)
        q_ref[:, hd * p + B_NOPE:(hd + 1) * p] = qr.astype(BF16)
    c0 = B_HEADS * p
    c = _rms(z[:, c0:c0 + B_KV_RANK], kvg_ref[...]).astype(BF16)
    kr = _rope_b(z[:, c0 + B_KV_RANK:c0 + B_KV_RANK + LANES], ck_ref, s1k_ref, s2k_ref).astype(BF16)
    kv = jnp.dot(c, wukv_ref[...], preferred_element_type=F32)
    for hd in range(B_HEADS):
        k_ref[:, hd * p:hd * p + B_NOPE] = kv[:, hd * p:hd * p + B_NOPE].astype(BF16)
        k_ref[:, hd * p + B_NOPE:(hd + 1) * p] = kr
        vt_ref[hd, 0] = kv[:, hd * p + B_NOPE:(hd + 1) * p].T.astype(BF16)


def _proj_b(h, w_b, kvg, w_ukv, tabs, seq, tm):
    t, dm = h.shape
    nw = w_b.shape[1]
    npos = seq // tm
    tab = pl.BlockSpec((tm, LANES), lambda i: (i % npos, 0))
    wide = B_HEADS * B_QK_PAD
    return pl.pallas_call(
        _proj_b_kernel,
        grid=(t // tm,),
        in_specs=[pl.BlockSpec((tm, dm), lambda i: (i, 0)),
                  _resident((dm, nw)),
                  _resident((1, B_KV_RANK)),
                  _resident(w_ukv.shape),
                  tab, tab, tab, tab, tab, tab],
        out_specs=[pl.BlockSpec((tm, wide), lambda i: (i, 0)),
                   pl.BlockSpec((tm, wide), lambda i: (i, 0)),
                   pl.BlockSpec((B_HEADS, 1, B_V, tm), lambda i: (0, i, 0, 0))],
        out_shape=[jax.ShapeDtypeStruct((t, wide), BF16),
                   jax.ShapeDtypeStruct((t, wide), BF16),
                   jax.ShapeDtypeStruct((B_HEADS, t // tm, B_V, tm), BF16)],
        name="proj_b",
        compiler_params=_params(),
    )(h, w_b, kvg, w_ukv, *tabs)


def _gates_kernel(h_ref, w_ref, o_ref):
    z = jnp.dot(h_ref[...], w_ref[...], preferred_element_type=F32)
    sig = 1.0 / (1.0 + jnp.exp(-z))
    lin = jnp.where(pl.program_id(0) == 0, z, jnp.ones_like(z))
    o_ref[...] = (sig * lin).astype(BF16)


def _gates(h, w_g, tm, tn):
    t, dm = h.shape
    n = w_g.shape[1]
    return pl.pallas_call(
        _gates_kernel,
        grid=(n // tn, t // tm),
        in_specs=[pl.BlockSpec((tm, dm), lambda j, i: (i, 0)),
                  pl.BlockSpec((dm, tn), lambda j, i: (0, j))],
        out_specs=pl.BlockSpec((tm, tn), lambda j, i: (i, j)),
        out_shape=jax.ShapeDtypeStruct((t, n), BF16),
        name="gates",
        compiler_params=_params(),
    )(h, w_g)


def _attn_kernel(q_ref, k_ref, vt_ref, g_ref, o_ref, qs_ref, s0_ref, s1_ref, p0_ref, p1_ref,
                 m_ref, l_ref, a_ref, acc_ref, *, group, dk, dv, tq, tk, n_chunks):
    for g in range(group):
        qs_ref[g * tq:(g + 1) * tq, :] = q_ref[:, g * dk:(g + 1) * dk]
    m_ref[...] = jnp.full(m_ref.shape, -jnp.inf, F32)
    l_ref[...] = jnp.zeros(l_ref.shape, F32)
    acc_ref[...] = jnp.zeros(acc_ref.shape, F32)

    def scores(j, s_ref):
        kc = k_ref[pl.ds(pl.multiple_of(j * tk, tk), tk), :]
        s_ref[...] = lax.dot_general(kc, qs_ref[...], (((1,), (1,)), ((), ())),
                                     preferred_element_type=F32)

    def values(j, p_ref):
        acc_ref[...] = a_ref[...] * acc_ref[...] + jnp.dot(
            vt_ref[j], p_ref[...], preferred_element_type=F32)

    def softmax(s_ref, p_ref):
        s = s_ref[...]
        m_old = m_ref[...]
        m_new = jnp.maximum(m_old, jnp.max(s, axis=0, keepdims=True))
        alpha = jnp.exp2(m_old - m_new)
        p = jnp.exp2(s - m_new)
        l_ref[...] = alpha * l_ref[...] + jnp.sum(p, axis=0, keepdims=True)
        p_ref[...] = p.astype(BF16)
        m_ref[...] = m_new
        return alpha

    def step(j, s_cur, s_nxt, p_cur, p_prv, first=False, final=False):
        if not final:
            scores(j + 1, s_nxt)
        alpha = softmax(s_cur, p_cur)
        if not first:
            values(j - 1, p_prv)
        a_ref[...] = alpha

    scores(0, s0_ref)
    sbuf = (s0_ref, s1_ref)
    pbuf = (p0_ref, p1_ref)
    for j in range(n_chunks):
        step(j, sbuf[j % 2], sbuf[1 - j % 2], pbuf[j % 2], pbuf[1 - j % 2],
             first=j == 0, final=j == n_chunks - 1)
    values(n_chunks - 1, pbuf[(n_chunks - 1) % 2])

    o = acc_ref[...] * (1.0 / l_ref[...])
    for g in range(group):
        og = o[:, g * tq:(g + 1) * tq].T
        gate = g_ref[:, g * dv:(g + 1) * dv].astype(F32)
        o_ref[:, g * dv:(g + 1) * dv] = (og * gate).astype(BF16)


def _attention(q, k, vt, gates, *, batch, seq, kv_heads, group, dk, dv, tq, tk, gate_col0):
    n_chunks = seq // tk
    assert n_chunks >= 2 and n_chunks % 2 == 0, (seq, tk)
    cols = group * tq
    q3 = q.reshape(batch, seq, q.shape[1])
    k3 = k.reshape(batch, seq, k.shape[1])
    g3 = gates.reshape(batch, seq, gates.shape[1])
    gw = group * dv
    gate_blk0 = gate_col0 // gw
    kern = functools.partial(_attn_kernel, group=group, dk=dk, dv=dv, tq=tq, tk=tk,
                             n_chunks=n_chunks)
    out = pl.pallas_call(
        kern,
        grid=(batch, kv_heads, seq // tq),
        in_specs=[pl.BlockSpec((None, tq, group * dk), lambda b, h, i: (b, i, h)),
                  pl.BlockSpec((None, seq, dk), lambda b, h, i: (b, 0, h)),
                  pl.BlockSpec((None, n_chunks, dv, tk), lambda b, h, i: (h, b, 0, 0)),
                  pl.BlockSpec((None, tq, gw), lambda b, h, i: (b, i, gate_blk0 + h))],
        out_specs=pl.BlockSpec((None, tq, gw), lambda b, h, i: (b, i, h)),
        out_shape=jax.ShapeDtypeStruct((batch, seq, kv_heads * gw), BF16),
        scratch_shapes=[pltpu.VMEM((cols, dk), BF16),
                        pltpu.VMEM((tk, cols), F32), pltpu.VMEM((tk, cols), F32),
                        pltpu.VMEM((tk, cols), BF16), pltpu.VMEM((tk, cols), BF16),
                        pltpu.VMEM((1, cols), F32), pltpu.VMEM((1, cols), F32),
                        pltpu.VMEM((1, cols), F32), pltpu.VMEM((dv, cols), F32)],
        name="attn_gqa" if group > 1 else "attn_mla",
        compiler_params=_params(),
    )(q3, k3, vt, g3)
    return out.reshape(batch * seq, kv_heads * gw)


def _out_kernel(oa_ref, ob_ref, ma_ref, mb_ref, x_ref, wa_ref, wb_ref, wo_ref, g_ref,
                *out_refs, last):
    pa = jnp.dot(oa_ref[...], wa_ref[...], preferred_element_type=F32)
    pb = jnp.dot(ob_ref[...], wb_ref[...], preferred_element_type=F32)
    merged = ma_ref[...].astype(F32) * pa + mb_ref[...].astype(F32) * pb
    y = x_ref[...] + jnp.dot(merged.astype(BF16), wo_ref[...], preferred_element_type=F32)
    if last:
        out_refs[0][...] = _rms(y, g_ref[...])
    else:
        out_refs[0][...] = y
        out_refs[1][...] = _rms(y, g_ref[...]).astype(BF16)


def _merge_out(oa, ob, gates, x, w_br_a, w_br_b, w_o, g_next, tm, last):
    t, dm = x.shape
    wa = oa.shape[1]
    mix0 = (A_WIDTH + B_WIDTH) // dm
    row = lambda i: (i, 0)
    out_specs = [pl.BlockSpec((tm, dm), row)]
    out_shape = [jax.ShapeDtypeStruct((t, dm), F32)]
    if not last:
        out_specs.append(pl.BlockSpec((tm, dm), row))
        out_shape.append(jax.ShapeDtypeStruct((t, dm), BF16))
    return pl.pallas_call(
        functools.partial(_out_kernel, last=last),
        grid=(t // tm,),
        in_specs=[pl.BlockSpec((tm, wa), row),
                  pl.BlockSpec((tm, wa), row),
                  pl.BlockSpec((tm, dm), lambda i: (i, mix0)),
                  pl.BlockSpec((tm, dm), lambda i: (i, mix0 + 1)),
                  pl.BlockSpec((tm, dm), row),
                  _resident(w_br_a.shape), _resident(w_br_b.shape), _resident(w_o.shape),
                  _resident((1, dm))],
        out_specs=out_specs,
        out_shape=out_shape,
        name="merge_out_final" if last else "merge_out",
        compiler_params=_params(),
    )(oa, ob, gates, gates, x, w_br_a, w_br_b, w_o, g_next)


def _axial_tables(n_tokens, rot_dim):
    rows = n_tokens // GRID_W
    row = jnp.repeat(jnp.arange(rows, dtype=F32), GRID_W)
    col = (jnp.arange(rows * GRID_W) % GRID_W).astype(F32)
    n_freq = rot_dim // 4
    inv = ROPE_THETA ** (-jnp.arange(n_freq, dtype=F32) / n_freq)
    ang = jnp.concatenate([row[:, None] * inv, col[:, None] * inv], axis=-1)
    return jnp.cos(ang), jnp.sin(ang)


def _rope_tables(seq):
    ca, sa = _axial_tables(seq, A_HEAD_DIM)
    c_a = jnp.concatenate([ca, ca], axis=-1)
    s_a = jnp.concatenate([-sa, sa], axis=-1)
    sc_a = A_HEAD_DIM ** -0.5 * LOG2_E
    tabs_a = (c_a * sc_a, s_a * sc_a, c_a, s_a)
    cb, sb = _axial_tables(seq, B_ROPE)
    z32 = jnp.zeros_like(cb)
    c_b = jnp.concatenate([cb, cb, z32, z32], axis=-1)
    s1_b = jnp.concatenate([-sb, z32, z32, z32], axis=-1)
    s2_b = jnp.concatenate([z32, sb, z32, z32], axis=-1)
    sc_b = B_QK ** -0.5 * LOG2_E
    tabs_b = (c_b * sc_b, s1_b * sc_b, s2_b * sc_b, c_b, s1_b, s2_b)
    return tabs_a, tabs_b


def _split_w_in(w_in):
    depth, dm, _ = w_in.shape
    kvw = A_KV_HEADS * A_HEAD_DIM
    sizes = (A_WIDTH, kvw, kvw, A_WIDTH, B_HEADS * B_QK, B_KV_RANK, B_ROPE, B_WIDTH, dm, dm)
    parts = []
    off = 0
    for s in sizes:
        parts.append(w_in[:, :, off:off + s])
        off += s
    qa, ka, va, gate_a, qb, ckv, krope, gate_b, mix_a, mix_b = parts
    w_a = jnp.concatenate([qa, ka, va], axis=-1).astype(BF16)
    qb = qb.reshape(depth, dm, B_HEADS, B_QK)
    qb = jnp.pad(qb, ((0, 0), (0, 0), (0, 0), (0, B_QK_PAD - B_QK)))
    qb = qb.reshape(depth, dm, B_HEADS * B_QK_PAD)
    krope = jnp.pad(krope, ((0, 0), (0, 0), (0, LANES - B_ROPE)))
    w_b = jnp.concatenate([qb, ckv, krope], axis=-1).astype(BF16)
    w_g = jnp.concatenate([gate_a, gate_b, mix_a, mix_b], axis=-1).astype(BF16)
    return w_a, w_b, w_g


def _tiles(seq):
    tm = min(512, seq // 2)
    tq_a = min(128, seq)
    tq_b = min(512, seq)
    return tm, tq_a, tq_b


def _trunk(x, weights, final_g):
    batch, seq, dm = x.shape
    (norm_g, w_a, w_b, w_g, q_norm_g, k_norm_g, kv_norm_g, w_ukv, w_branch, w_out) = weights
    depth = norm_g.shape[0]
    tm, tq_a, tq_b = _tiles(seq)
    tabs_a, tabs_b = _rope_tables(seq)
    xt = x.reshape(batch * seq, dm)
    h = _input_norm(xt, norm_g[0][None, :], tm)
    for l in range(depth):
        q_a, k_a, vt_a = _proj_a(h, w_a[l], q_norm_g[l][None, :], k_norm_g[l][None, :],
                                 tabs_a, seq, tm)
        q_b, k_b, vt_b = _proj_b(h, w_b[l], kv_norm_g[l][None, :], w_ukv[l], tabs_b, seq, tm)
        gates = _gates(h, w_g[l], tm, dm)
        oa = _attention(q_a, k_a, vt_a, gates, batch=batch, seq=seq, kv_heads=A_KV_HEADS,
                        group=A_GROUP, dk=A_HEAD_DIM, dv=A_HEAD_DIM, tq=tq_a, tk=tm,
                        gate_col0=0)
        ob = _attention(q_b, k_b, vt_b, gates, batch=batch, seq=seq, kv_heads=B_HEADS,
                        group=1, dk=B_QK_PAD, dv=B_V, tq=tq_b, tk=tm, gate_col0=A_WIDTH)
        last = l == depth - 1
        g_next = final_g if last else norm_g[l + 1]
        res = _merge_out(oa, ob, gates, xt, w_branch[l, 0], w_branch[l, 1], w_out[l],
                         g_next[None, :], min(256, tm), last)
        if last:
            xt = res[0]
        else:
            xt, h = res
    return xt.reshape(batch, seq, dm)


def kernel(x_prompt, x_sample, norm_g, w_in, q_norm_g, k_norm_g, kv_norm_g, w_ukv, w_branch,
           w_out, final_norm_g):
    w_a, w_b, w_g = _split_w_in(w_in)
    weights = (norm_g, w_a, w_b, w_g, q_norm_g, k_norm_g, kv_norm_g, w_ukv.astype(BF16),
               w_branch.astype(BF16), w_out.astype(BF16))
    y_prompt = _trunk(x_prompt, weights, final_norm_g)
    y_sample = _trunk(x_sample, weights, final_norm_g)
    return (y_prompt, y_sample)
```

```python
import functools

import jax
import jax.numpy as jnp
from jax import lax
from jax.experimental import pallas as pl
from jax.experimental.pallas import tpu as pltpu

F32 = jnp.float32
BF16 = jnp.bfloat16

GRID_W = 64
ROPE_THETA = 10000.0
EPS = 1e-6
A_HEADS = 8
A_KV_HEADS = 2
A_HEAD_DIM = 128
A_GROUP = A_HEADS // A_KV_HEADS
A_WIDTH = A_HEADS * A_HEAD_DIM
B_HEADS = 8
B_NOPE = 128
B_ROPE = 64
B_QK = B_NOPE + B_ROPE
B_QK_PAD = 256
B_V = 128
B_KV_RANK = 512
B_WIDTH = B_HEADS * B_V
LANES = 128
LOG2_E = 1.4426950408889634

VMEM_LIMIT = 56 * 1024 * 1024

TK = 512
TQ_A = 256
TQ_B = 1024
N_SUB = 2
SUM_ROWS = 16


def _params(**kw):
    return pltpu.CompilerParams(vmem_limit_bytes=VMEM_LIMIT, **kw)


def _resident(shape):
    nd = len(shape)
    return pl.BlockSpec(shape, lambda *_: (0,) * nd, pipeline_mode=pl.Buffered(1))


def _rms(z, g):
    return z * lax.rsqrt(jnp.mean(z * z, axis=-1, keepdims=True) + EPS) * g


def _norm_kernel(x_ref, g_ref, h_ref):
    h_ref[...] = _rms(x_ref[...], g_ref[...]).astype(BF16)


def _input_norm(x, g, tm):
    t, d = x.shape
    return pl.pallas_call(
        _norm_kernel,
        grid=(t // tm,),
        in_specs=[pl.BlockSpec((tm, d), lambda i: (i, 0)),
                  pl.BlockSpec((1, d), lambda i: (0, 0))],
        out_specs=pl.BlockSpec((tm, d), lambda i: (i, 0)),
        out_shape=jax.ShapeDtypeStruct((t, d), BF16),
        name="input_norm",
        compiler_params=_params(),
    )(x, g)


def _proj_a_kernel(h_ref, w_ref, qg_ref, kg_ref, cq_ref, sq_ref, ck_ref, sk_ref,
                   q_ref, k_ref, vt_ref):
    z = jnp.dot(h_ref[...], w_ref[...], preferred_element_type=F32)
    d = A_HEAD_DIM
    for hd in range(A_HEADS):
        y = _rms(z[:, hd * d:(hd + 1) * d], qg_ref[...])
        y = y * cq_ref[...] + pltpu.roll(y, d // 2, 1) * sq_ref[...]
        q_ref[:, hd * d:(hd + 1) * d] = y.astype(BF16)
    k0 = A_WIDTH
    v0 = A_WIDTH + A_KV_HEADS * d
    for hd in range(A_KV_HEADS):
        y = _rms(z[:, k0 + hd * d:k0 + (hd + 1) * d], kg_ref[...])
        y = y * ck_ref[...] + pltpu.roll(y, d // 2, 1) * sk_ref[...]
        k_ref[:, hd * d:(hd + 1) * d] = y.astype(BF16)
        vt_ref[hd, 0] = z[:, v0 + hd * d:v0 + (hd + 1) * d].T.astype(BF16)


def _proj_a(h, w_a, qg, kg, tabs, seq, tm):
    t, dm = h.shape
    nw = w_a.shape[1]
    npos = seq // tm
    tab = pl.BlockSpec((tm, LANES), lambda i: (i % npos, 0))
    return pl.pallas_call(
        _proj_a_kernel,
        grid=(t // tm,),
        in_specs=[pl.BlockSpec((tm, dm), lambda i: (i, 0)),
                  _resident((dm, nw)),
                  _resident((1, LANES)), _resident((1, LANES)),
                  tab, tab, tab, tab],
        out_specs=[pl.BlockSpec((tm, A_WIDTH), lambda i: (i, 0)),
                   pl.BlockSpec((tm, A_KV_HEADS * A_HEAD_DIM), lambda i: (i, 0)),
                   pl.BlockSpec((A_KV_HEADS, 1, A_HEAD_DIM, tm), lambda i: (0, i, 0, 0))],
        out_shape=[jax.ShapeDtypeStruct((t, A_WIDTH), BF16),
                   jax.ShapeDtypeStruct((t, A_KV_HEADS * A_HEAD_DIM), BF16),
                   jax.ShapeDtypeStruct((A_KV_HEADS, t // tm, A_HEAD_DIM, tm), BF16)],
        name="proj_a",
        compiler_params=_params(),
    )(h, w_a, qg, kg, *tabs)


def _rope_b(y, c_ref, s1_ref, s2_ref):
    return (y * c_ref[...] + pltpu.roll(y, LANES - B_ROPE // 2, 1) * s1_ref[...]
            + pltpu.roll(y, B_ROPE // 2, 1) * s2_ref[...])


def _proj_b_kernel(h_ref, w_ref, kvg_ref, wukv_ref, cq_ref, s1q_ref, s2q_ref,
                   ck_ref, s1k_ref, s2k_ref, q_ref, k_ref, vt_ref):
    z = jnp.dot(h_ref[...], w_ref[...], preferred_element_type=F32)
    scale = B_QK ** -0.5 * LOG2_E
    p = B_QK_PAD
    for hd in range(B_HEADS):
        q_ref[:, hd * p:hd * p + B_NOPE] = (z[:, hd * p:hd * p + B_NOPE] * scale).astype(BF16)
        qr = _rope_b(z[:, hd * p + B_NOPE:(hd + 1) * p], cq_ref, s1q_ref, s2q_ref)
        q_ref[:, hd * p + B_NOPE:(hd + 1) * p] = qr.astype(BF16)
    c0 = B_HEADS * p
    c = _rms(z[:, c0:c0 + B_KV_RANK], kvg_ref[...]).astype(BF16)
    kr = _rope_b(z[:, c0 + B_KV_RANK:c0 + B_KV_RANK + LANES], ck_ref, s1k_ref, s2k_ref).astype(BF16)
    kv = jnp.dot(c, wukv_ref[...], preferred_element_type=F32)
    for hd in range(B_HEADS):
        k_ref[:, hd * p:hd * p + B_NOPE] = kv[:, hd * p:hd * p + B_NOPE].astype(BF16)
        k_ref[:, hd * p + B_NOPE:(hd + 1) * p] = kr
        vt_ref[hd, 0] = kv[:, hd * p + B_NOPE:(hd + 1) * p].T.astype(BF16)


def _proj_b(h, w_b, kvg, w_ukv, tabs, seq, tm):
    t, dm = h.shape
    nw = w_b.shape[1]
    npos = seq // tm
    tab = pl.BlockSpec((tm, LANES), lambda i: (i % npos, 0))
    wide = B_HEADS * B_QK_PAD
    return pl.pallas_call(
        _proj_b_kernel,
        grid=(t // tm,),
        in_specs=[pl.BlockSpec((tm, dm), lambda i: (i, 0)),
                  _resident((dm, nw)),
                  _resident((1, B_KV_RANK)),
                  _resident(w_ukv.shape),
                  tab, tab, tab, tab, tab, tab],
        out_specs=[pl.BlockSpec((tm, wide), lambda i: (i, 0)),
                   pl.BlockSpec((tm, wide), lambda i: (i, 0)),
                   pl.BlockSpec((B_HEADS, 1, B_V, tm), lambda i: (0, i, 0, 0))],
        out_shape=[jax.ShapeDtypeStruct((t, wide), BF16),
                   jax.ShapeDtypeStruct((t, wide), BF16),
                   jax.ShapeDtypeStruct((B_HEADS, t // tm, B_V, tm), BF16)],
        name="proj_b",
        compiler_params=_params(),
    )(h, w_b, kvg, w_ukv, *tabs)


def _gates_kernel(h_ref, w_ref, o_ref):
    z = jnp.dot(h_ref[...], w_ref[...], preferred_element_type=F32)
    sig = 1.0 / (1.0 + jnp.exp(-z))
    lin = jnp.where(pl.program_id(0) == 0, z, jnp.ones_like(z))
    o_ref[...] = (sig * lin).astype(BF16)


def _gates(h, w_g, tm, tn):
    t, dm = h.shape
    n = w_g.shape[1]
    return pl.pallas_call(
        _gates_kernel,
        grid=(n // tn, t // tm),
        in_specs=[pl.BlockSpec((tm, dm), lambda j, i: (i, 0)),
                  pl.BlockSpec((dm, tn), lambda j, i: (0, j))],
        out_specs=pl.BlockSpec((tm, tn), lambda j, i: (i, j)),
        out_shape=jax.ShapeDtypeStruct((t, n), BF16),
        name="gates",
        compiler_params=_params(),
    )(h, w_g)


def _attn_kernel(q_ref, k_ref, vt_ref, g_ref, o_ref, qs_ref, s_ref, p_ref,
                 m_ref, a_ref, acc_ref, *, group, dk, dv, tq, tk, n_chunks, n_sub):
    tc = vt_ref.shape[-1]
    w = min(tk, tc)
    items = [(sb, j) for sb in range(n_sub) for j in range(n_chunks)]

    def scores(t):
        sb, j = items[t]
        qs = qs_ref.at[sb % 2]
        if j == 0:
            r0 = sb * tq
            for g in range(group):
                qs[g * tq:(g + 1) * tq, :] = q_ref[r0:r0 + tq, g * dk:(g + 1) * dk]
        s_ref[t % 2] = lax.dot_general(k_ref[j * tk:(j + 1) * tk, :], qs[...],
                                       (((1,), (1,)), ((), ())),
                                       preferred_element_type=F32)

    def softmax(t):
        sb, j = items[t]
        st = sb % 2
        s = s_ref[t % 2]
        if j == 0:
            m_new = jnp.max(s, axis=0, keepdims=True)
            alpha = jnp.zeros_like(m_new)
        else:
            m_old = m_ref[st]
            m_new = jnp.maximum(m_old, jnp.max(s, axis=0, keepdims=True))
            alpha = jnp.exp2(m_old - m_new)
        p_ref[t % 2] = jnp.exp2(s - m_new).astype(BF16)
        m_ref[st] = m_new
        return alpha

    def values(t):
        sb, j = items[t]
        st = sb % 2
        ones = jnp.ones((SUM_ROWS, w), BF16)
        pv = None
        for off in range(0, tk, w):
            row = j * tk + off
            vt1 = jnp.concatenate([vt_ref[row // tc, :, row % tc:row % tc + w], ones], axis=0)
            part = jnp.dot(vt1, p_ref[t % 2, off:off + w, :],
                           preferred_element_type=F32)
            pv = part if pv is None else pv + part
        if j == 0:
            acc_ref[st] = pv
        else:
            acc_ref[st] = a_ref[st] * acc_ref[st] + pv
        if j == n_chunks - 1:
            r0 = sb * tq
            o = acc_ref[st, 0:dv, :] * (1.0 / acc_ref[st, dv:dv + 1, :])
            for g in range(group):
                og = o[:, g * tq:(g + 1) * tq].T
                gate = g_ref[r0:r0 + tq, g * dv:(g + 1) * dv].astype(F32)
                o_ref[r0:r0 + tq, g * dv:(g + 1) * dv] = (og * gate).astype(BF16)

    scores(0)
    for t in range(len(items)):
        if t + 1 < len(items):
            scores(t + 1)
        alpha = softmax(t)
        if t > 0:
            values(t - 1)
        a_ref[items[t][0] % 2] = alpha
    values(len(items) - 1)


def _attention(q, k, vt, gates, *, batch, seq, kv_heads, group, dk, dv, tq, tk, n_sub, gate_col0):
    tc = vt.shape[-1]
    n_chunks = seq // tk
    assert seq % tk == 0 and n_chunks >= 2 and (tk % tc == 0 or tc % tk == 0), (seq, tk, tc)
    cols = group * tq
    tb = n_sub * tq
    q3 = q.reshape(batch, seq, q.shape[1])
    k3 = k.reshape(batch, seq, k.shape[1])
    g3 = gates.reshape(batch, seq, gates.shape[1])
    gw = group * dv
    gate_blk0 = gate_col0 // gw
    kern = functools.partial(_attn_kernel, group=group, dk=dk, dv=dv, tq=tq, tk=tk,
                             n_chunks=n_chunks, n_sub=n_sub)
    out = pl.pallas_call(
        kern,
        grid=(batch, kv_heads, seq // tb),
        in_specs=[pl.BlockSpec((None, tb, group * dk), lambda b, h, i: (b, i, h)),
                  pl.BlockSpec((None, seq, dk), lambda b, h, i: (b, 0, h)),
                  pl.BlockSpec((None, seq // tc, dv, tc), lambda b, h, i: (h, b, 0, 0)),
                  pl.BlockSpec((None, tb, gw), lambda b, h, i: (b, i, gate_blk0 + h))],
        out_specs=pl.BlockSpec((None, tb, gw), lambda b, h, i: (b, i, h)),
        out_shape=jax.ShapeDtypeStruct((batch, seq, kv_heads * gw), BF16),
        scratch_shapes=[pltpu.VMEM((2, cols, dk), BF16),
                        pltpu.VMEM((2, tk, cols), F32),
                        pltpu.VMEM((2, tk, cols), BF16),
                        pltpu.VMEM((2, 1, cols), F32), pltpu.VMEM((2, 1, cols), F32),
                        pltpu.VMEM((2, dv + SUM_ROWS, cols), F32)],
        name="attn_gqa" if group > 1 else "attn_mla",
        compiler_params=_params(),
    )(q3, k3, vt, g3)
    return out.reshape(batch * seq, kv_heads * gw)


def _out_kernel(oa_ref, ob_ref, ma_ref, mb_ref, x_ref, wa_ref, wb_ref, wo_ref, g_ref,
                *out_refs, last):
    pa = jnp.dot(oa_ref[...], wa_ref[...], preferred_element_type=F32)
    pb = jnp.dot(ob_ref[...], wb_ref[...], preferred_element_type=F32)
    merged = ma_ref[...].astype(F32) * pa + mb_ref[...].astype(F32) * pb
    y = x_ref[...] + jnp.dot(merged.astype(BF16), wo_ref[...], preferred_element_type=F32)
    if last:
        out_refs[0][...] = _rms(y, g_ref[...])
    else:
        out_refs[0][...] = y
        out_refs[1][...] = _rms(y, g_ref[...]).astype(BF16)


def _merge_out(oa, ob, gates, x, w_br_a, w_br_b, w_o, g_next, tm, last):
    t, dm = x.shape
    wa = oa.shape[1]
    mix0 = (A_WIDTH + B_WIDTH) // dm
    row = lambda i: (i, 0)
    out_specs = [pl.BlockSpec((tm, dm), row)]
    out_shape = [jax.ShapeDtypeStruct((t, dm), F32)]
    if not last:
        out_specs.append(pl.BlockSpec((tm, dm), row))
        out_shape.append(jax.ShapeDtypeStruct((t, dm), BF16))
    return pl.pallas_call(
        functools.partial(_out_kernel, last=last),
        grid=(t // tm,),
        in_specs=[pl.BlockSpec((tm, wa), row),
                  pl.BlockSpec((tm, wa), row),
                  pl.BlockSpec((tm, dm), lambda i: (i, mix0)),
                  pl.BlockSpec((tm, dm), lambda i: (i, mix0 + 1)),
                  pl.BlockSpec((tm, dm), row),
                  _resident(w_br_a.shape), _resident(w_br_b.shape), _resident(w_o.shape),
                  _resident((1, dm))],
        out_specs=out_specs,
        out_shape=out_shape,
        name="merge_out_final" if last else "merge_out",
        compiler_params=_params(),
    )(oa, ob, gates, gates, x, w_br_a, w_br_b, w_o, g_next)


def _axial_tables(n_tokens, rot_dim):
    rows = n_tokens // GRID_W
    row = jnp.repeat(jnp.arange(rows, dtype=F32), GRID_W)
    col = (jnp.arange(rows * GRID_W) % GRID_W).astype(F32)
    n_freq = rot_dim // 4
    inv = ROPE_THETA ** (-jnp.arange(n_freq, dtype=F32) / n_freq)
    ang = jnp.concatenate([row[:, None] * inv, col[:, None] * inv], axis=-1)
    return jnp.cos(ang), jnp.sin(ang)


def _rope_tables(seq):
    ca, sa = _axial_tables(seq, A_HEAD_DIM)
    c_a = jnp.concatenate([ca, ca], axis=-1)
    s_a = jnp.concatenate([-sa, sa], axis=-1)
    sc_a = A_HEAD_DIM ** -0.5 * LOG2_E
    tabs_a = (c_a * sc_a, s_a * sc_a, c_a, s_a)
    cb, sb = _axial_tables(seq, B_ROPE)
    z32 = jnp.zeros_like(cb)
    c_b = jnp.concatenate([cb, cb, z32, z32], axis=-1)
    s1_b = jnp.concatenate([-sb, z32, z32, z32], axis=-1)
    s2_b = jnp.concatenate([z32, sb, z32, z32], axis=-1)
    sc_b = B_QK ** -0.5 * LOG2_E
    tabs_b = (c_b * sc_b, s1_b * sc_b, s2_b * sc_b, c_b, s1_b, s2_b)
    return tabs_a, tabs_b


def _split_w_in(w_in):
    depth, dm, _ = w_in.shape
    kvw = A_KV_HEADS * A_HEAD_DIM
    sizes = (A_WIDTH, kvw, kvw, A_WIDTH, B_HEADS * B_QK, B_KV_RANK, B_ROPE, B_WIDTH, dm, dm)
    parts = []
    off = 0
    for s in sizes:
        parts.append(w_in[:, :, off:off + s])
        off += s
    qa, ka, va, gate_a, qb, ckv, krope, gate_b, mix_a, mix_b = parts
    w_a = jnp.concatenate([qa, ka, va], axis=-1).astype(BF16)
    qb = qb.reshape(depth, dm, B_HEADS, B_QK)
    qb = jnp.pad(qb, ((0, 0), (0, 0), (0, 0), (0, B_QK_PAD - B_QK)))
    qb = qb.reshape(depth, dm, B_HEADS * B_QK_PAD)
    krope = jnp.pad(krope, ((0, 0), (0, 0), (0, LANES - B_ROPE)))
    w_b = jnp.concatenate([qb, ckv, krope], axis=-1).astype(BF16)
    w_g = jnp.concatenate([gate_a, gate_b, mix_a, mix_b], axis=-1).astype(BF16)
    return w_a, w_b, w_g


def _tiles(seq):
    tm = min(512, seq // 2)
    tk = min(TK, seq // 2)
    tq_a = min(TQ_A, seq)
    tq_b = min(TQ_B, seq)
    n_sub_a = min(N_SUB, seq // tq_a)
    n_sub_b = min(N_SUB, seq // tq_b)
    return tm, tk, tq_a, tq_b, n_sub_a, n_sub_b


def _trunk(x, weights, final_g):
    batch, seq, dm = x.shape
    (norm_g, w_a, w_b, w_g, q_norm_g, k_norm_g, kv_norm_g, w_ukv, w_branch, w_out) = weights
    depth = norm_g.shape[0]
    tm, tk, tq_a, tq_b, n_sub_a, n_sub_b = _tiles(seq)
    tabs_a, tabs_b = _rope_tables(seq)
    xt = x.reshape(batch * seq, dm)
    h = _input_norm(xt, norm_g[0][None, :], tm)
    for l in range(depth):
        q_a, k_a, vt_a = _proj_a(h, w_a[l], q_norm_g[l][None, :], k_norm_g[l][None, :],
                                 tabs_a, seq, tm)
        q_b, k_b, vt_b = _proj_b(h, w_b[l], kv_norm_g[l][None, :], w_ukv[l], tabs_b, seq, tm)
        gates = _gates(h, w_g[l], tm, dm)
        oa = _attention(q_a, k_a, vt_a, gates, batch=batch, seq=seq, kv_heads=A_KV_HEADS,
                        group=A_GROUP, dk=A_HEAD_DIM, dv=A_HEAD_DIM, tq=tq_a, tk=tk,
                        n_sub=n_sub_a, gate_col0=0)
        ob = _attention(q_b, k_b, vt_b, gates, batch=batch, seq=seq, kv_heads=B_HEADS,
                        group=1, dk=B_QK_PAD, dv=B_V, tq=tq_b, tk=tk, n_sub=n_sub_b,
                        gate_col0=A_WIDTH)
        last = l == depth - 1
        g_next = final_g if last else norm_g[l + 1]
        res = _merge_out(oa, ob, gates, xt, w_branch[l, 0], w_branch[l, 1], w_out[l],
                         g_next[None, :], min(256, tm), last)
        if last:
            xt = res[0]
        else:
            xt, h = res
    return xt.reshape(batch, seq, dm)


def kernel(x_prompt, x_sample, norm_g, w_in, q_norm_g, k_norm_g, kv_norm_g, w_ukv, w_branch,
           w_out, final_norm_g):
    w_a, w_b, w_g = _split_w_in(w_in)
    weights = (norm_g, w_a, w_b, w_g, q_norm_g, k_norm_g, kv_norm_g, w_ukv.astype(BF16),
               w_branch.astype(BF16), w_out.astype(BF16))
    y_prompt = _trunk(x_prompt, weights, final_norm_g)
    y_sample = _trunk(x_sample, weights, final_norm_g)
    return (y_prompt, y_sample)
```

```python
import functools

import jax
import jax.numpy as jnp
from jax import lax
from jax.experimental import pallas as pl
from jax.experimental.pallas import tpu as pltpu

F32 = jnp.float32
BF16 = jnp.bfloat16

GRID_W = 64
ROPE_THETA = 10000.0
EPS = 1e-6
A_HEADS = 8
A_KV_HEADS = 2
A_HEAD_DIM = 128
A_GROUP = A_HEADS // A_KV_HEADS
A_WIDTH = A_HEADS * A_HEAD_DIM
B_HEADS = 8
B_NOPE = 128
B_ROPE = 64
B_QK = B_NOPE + B_ROPE
B_QK_PAD = 256
B_V = 128
B_KV_RANK = 512
B_WIDTH = B_HEADS * B_V
LANES = 128
LOG2_E = 1.4426950408889634

VMEM_LIMIT = 56 * 1024 * 1024

TK = 512
TQ_A = 256
TQ_B = 1024
N_SUB = 4
ROW_SPLIT_A = 4
ROW_SPLIT_B = 2
ROW_SPLIT_G = 4
SUM_ROWS = 16


def _params(**kw):
    return pltpu.CompilerParams(vmem_limit_bytes=VMEM_LIMIT, **kw)


def _resident(shape):
    nd = len(shape)
    return pl.BlockSpec(shape, lambda *_: (0,) * nd, pipeline_mode=pl.Buffered(1))


def _rms(z, g):
    return z * lax.rsqrt(jnp.mean(z * z, axis=-1, keepdims=True) + EPS) * g


def _norm_kernel(x_ref, g_ref, h_ref):
    h_ref[...] = _rms(x_ref[...], g_ref[...]).astype(BF16)


def _input_norm(x, g, tm):
    t, d = x.shape
    return pl.pallas_call(
        _norm_kernel,
        grid=(t // tm,),
        in_specs=[pl.BlockSpec((tm, d), lambda i: (i, 0)),
                  pl.BlockSpec((1, d), lambda i: (0, 0))],
        out_specs=pl.BlockSpec((tm, d), lambda i: (i, 0)),
        out_shape=jax.ShapeDtypeStruct((t, d), BF16),
        name="input_norm",
        compiler_params=_params(),
    )(x, g)


def _proj_a_kernel(h_ref, w_ref, qg_ref, kg_ref, cq_ref, sq_ref, ck_ref, sk_ref,
                   q_ref, k_ref, vt_ref):
    d = A_HEAD_DIM
    k0 = A_WIDTH
    v0 = A_WIDTH + A_KV_HEADS * d
    tr = h_ref.shape[0] // ROW_SPLIT_A
    for r in range(ROW_SPLIT_A):
        rows = slice(r * tr, (r + 1) * tr)
        z = jnp.dot(h_ref[rows, :], w_ref[...], preferred_element_type=F32)
        for hd in range(A_HEADS):
            y = _rms(z[:, hd * d:(hd + 1) * d], qg_ref[...])
            y = y * cq_ref[rows, :] + pltpu.roll(y, d // 2, 1) * sq_ref[rows, :]
            q_ref[rows, hd * d:(hd + 1) * d] = y.astype(BF16)
        for hd in range(A_KV_HEADS):
            y = _rms(z[:, k0 + hd * d:k0 + (hd + 1) * d], kg_ref[...])
            y = y * ck_ref[rows, :] + pltpu.roll(y, d // 2, 1) * sk_ref[rows, :]
            k_ref[rows, hd * d:(hd + 1) * d] = y.astype(BF16)
            vt_ref[hd, 0, :, rows] = z[:, v0 + hd * d:v0 + (hd + 1) * d].T.astype(BF16)


def _proj_a(h, w_a, qg, kg, tabs, seq, tm):
    t, dm = h.shape
    nw = w_a.shape[1]
    npos = seq // tm
    tab = pl.BlockSpec((tm, LANES), lambda i: (i % npos, 0))
    return pl.pallas_call(
        _proj_a_kernel,
        grid=(t // tm,),
        in_specs=[pl.BlockSpec((tm, dm), lambda i: (i, 0)),
                  _resident((dm, nw)),
                  _resident((1, LANES)), _resident((1, LANES)),
                  tab, tab, tab, tab],
        out_specs=[pl.BlockSpec((tm, A_WIDTH), lambda i: (i, 0)),
                   pl.BlockSpec((tm, A_KV_HEADS * A_HEAD_DIM), lambda i: (i, 0)),
                   pl.BlockSpec((A_KV_HEADS, 1, A_HEAD_DIM, tm), lambda i: (0, i, 0, 0))],
        out_shape=[jax.ShapeDtypeStruct((t, A_WIDTH), BF16),
                   jax.ShapeDtypeStruct((t, A_KV_HEADS * A_HEAD_DIM), BF16),
                   jax.ShapeDtypeStruct((A_KV_HEADS, t // tm, A_HEAD_DIM, tm), BF16)],
        name="proj_a",
        compiler_params=_params(),
    )(h, w_a, qg, kg, *tabs)


def _rope_b(y, c, s1, s2):
    return (y * c + pltpu.roll(y, LANES - B_ROPE // 2, 1) * s1
            + pltpu.roll(y, B_ROPE // 2, 1) * s2)


def _proj_b_kernel(h_ref, w_ref, kvg_ref, wukv_ref, cq_ref, s1q_ref, s2q_ref,
                   ck_ref, s1k_ref, s2k_ref, q_ref, k_ref, vt_ref):
    scale = B_QK ** -0.5 * LOG2_E
    p = B_QK_PAD
    r0 = B_HEADS * B_NOPE
    c0 = r0 + B_HEADS * B_ROPE
    tr = h_ref.shape[0] // ROW_SPLIT_B
    for r in range(ROW_SPLIT_B):
        rows = slice(r * tr, (r + 1) * tr)
        z = jnp.dot(h_ref[rows, :], w_ref[...], preferred_element_type=F32)
        c = _rms(z[:, c0:c0 + B_KV_RANK], kvg_ref[...]).astype(BF16)
        kv = jnp.dot(c, wukv_ref[...], preferred_element_type=F32)
        kr = _rope_b(z[:, c0 + B_KV_RANK:c0 + B_KV_RANK + LANES],
                     ck_ref[rows, :], s1k_ref[rows, :], s2k_ref[rows, :]).astype(BF16)
        for hd in range(B_HEADS):
            q_ref[rows, hd * p:hd * p + B_NOPE] = (
                z[:, hd * B_NOPE:(hd + 1) * B_NOPE] * scale).astype(BF16)
            pair = z[:, r0 + (hd // 2) * LANES:r0 + (hd // 2 + 1) * LANES]
            if hd % 2:
                pair = pltpu.roll(pair, LANES // 2, 1)
            qr = _rope_b(pair, cq_ref[rows, :], s1q_ref[rows, :], s2q_ref[rows, :])
            q_ref[rows, hd * p + B_NOPE:(hd + 1) * p] = qr.astype(BF16)
            k_ref[rows, hd * p:hd * p + B_NOPE] = kv[:, hd * p:hd * p + B_NOPE].astype(BF16)
            k_ref[rows, hd * p + B_NOPE:(hd + 1) * p] = kr
            vt_ref[hd, 0, :, rows] = kv[:, hd * p + B_NOPE:(hd + 1) * p].T.astype(BF16)


def _proj_b(h, w_b, kvg, w_ukv, tabs, seq, tm):
    t, dm = h.shape
    nw = w_b.shape[1]
    npos = seq // tm
    tab = pl.BlockSpec((tm, LANES), lambda i: (i % npos, 0))
    wide = B_HEADS * B_QK_PAD
    return pl.pallas_call(
        _proj_b_kernel,
        grid=(t // tm,),
        in_specs=[pl.BlockSpec((tm, dm), lambda i: (i, 0)),
                  _resident((dm, nw)),
                  _resident((1, B_KV_RANK)),
                  _resident(w_ukv.shape),
                  tab, tab, tab, tab, tab, tab],
        out_specs=[pl.BlockSpec((tm, wide), lambda i: (i, 0)),
                   pl.BlockSpec((tm, wide), lambda i: (i, 0)),
                   pl.BlockSpec((B_HEADS, 1, B_V, tm), lambda i: (0, i, 0, 0))],
        out_shape=[jax.ShapeDtypeStruct((t, wide), BF16),
                   jax.ShapeDtypeStruct((t, wide), BF16),
                   jax.ShapeDtypeStruct((B_HEADS, t // tm, B_V, tm), BF16)],
        name="proj_b",
        compiler_params=_params(),
    )(h, w_b, kvg, w_ukv, *tabs)


def _gates_kernel(h_ref, w_ref, o_ref):
    tr = h_ref.shape[0] // ROW_SPLIT_G
    for r in range(ROW_SPLIT_G):
        rows = slice(r * tr, (r + 1) * tr)
        z = jnp.dot(h_ref[rows, :], w_ref[...], preferred_element_type=F32)
        sig = 1.0 / (1.0 + jnp.exp(-z))
        lin = jnp.where(pl.program_id(0) == 0, z, jnp.ones_like(z))
        o_ref[rows, :] = (sig * lin).astype(BF16)


def _gates(h, w_g, tm, tn):
    t, dm = h.shape
    n = w_g.shape[1]
    return pl.pallas_call(
        _gates_kernel,
        grid=(n // tn, t // tm),
        in_specs=[pl.BlockSpec((tm, dm), lambda j, i: (i, 0)),
                  pl.BlockSpec((dm, tn), lambda j, i: (0, j))],
        out_specs=pl.BlockSpec((tm, tn), lambda j, i: (i, j)),
        out_shape=jax.ShapeDtypeStruct((t, n), BF16),
        name="gates",
        compiler_params=_params(),
    )(h, w_g)


def _attn_kernel(q_ref, k_ref, vt_ref, g_ref, o_ref, qs_ref, s_ref, p_ref,
                 m_ref, a_ref, acc_ref, *, group, dk, dv, tq, tk, n_chunks, n_sub):
    tc = vt_ref.shape[-1]
    w = min(tk, tc)
    items = [(sb, j) for sb in range(n_sub) for j in range(n_chunks)]

    def scores(t):
        sb, j = items[t]
        qs = qs_ref.at[sb % 2]
        if j == 0:
            r0 = sb * tq
            for g in range(group):
                qs[g * tq:(g + 1) * tq, :] = q_ref[r0:r0 + tq, g * dk:(g + 1) * dk]
        s_ref[t % 2] = lax.dot_general(k_ref[j * tk:(j + 1) * tk, :], qs[...],
                                       (((1,), (1,)), ((), ())),
                                       preferred_element_type=F32)

    def softmax(t):
        sb, j = items[t]
        st = sb % 2
        s = s_ref[t % 2]
        if j == 0:
            m_new = jnp.max(s, axis=0, keepdims=True)
            alpha = jnp.zeros_like(m_new)
        else:
            m_old = m_ref[st]
            m_new = jnp.maximum(m_old, jnp.max(s, axis=0, keepdims=True))
            alpha = jnp.exp2(m_old - m_new)
        p_ref[t % 2] = jnp.exp2(s - m_new).astype(BF16)
        m_ref[st] = m_new
        return alpha

    def values(t):
        sb, j = items[t]
        st = sb % 2
        ones = jnp.ones((SUM_ROWS, w), BF16)
        pv = None
        for off in range(0, tk, w):
            row = j * tk + off
            vt1 = jnp.concatenate([vt_ref[row // tc, :, row % tc:row % tc + w], ones], axis=0)
            part = jnp.dot(vt1, p_ref[t % 2, off:off + w, :],
                           preferred_element_type=F32)
            pv = part if pv is None else pv + part
        if j == 0:
            acc_ref[st] = pv
        else:
            acc_ref[st] = a_ref[st] * acc_ref[st] + pv
        if j == n_chunks - 1:
            r0 = sb * tq
            o = acc_ref[st, 0:dv, :] * (1.0 / acc_ref[st, dv:dv + 1, :])
            for g in range(group):
                og = o[:, g * tq:(g + 1) * tq].T
                gate = g_ref[r0:r0 + tq, g * dv:(g + 1) * dv].astype(F32)
                o_ref[r0:r0 + tq, g * dv:(g + 1) * dv] = (og * gate).astype(BF16)

    scores(0)
    for t in range(len(items)):
        if t + 1 < len(items):
            scores(t + 1)
        alpha = softmax(t)
        if t > 0:
            values(t - 1)
        a_ref[items[t][0] % 2] = alpha
    values(len(items) - 1)


def _attention(q, k, vt, gates, *, batch, seq, kv_heads, group, dk, dv, tq, tk, n_sub, gate_col0):
    tc = vt.shape[-1]
    n_chunks = seq // tk
    assert seq % tk == 0 and n_chunks >= 2 and (tk % tc == 0 or tc % tk == 0), (seq, tk, tc)
    cols = group * tq
    tb = n_sub * tq
    q3 = q.reshape(batch, seq, q.shape[1])
    k3 = k.reshape(batch, seq, k.shape[1])
    g3 = gates.reshape(batch, seq, gates.shape[1])
    gw = group * dv
    gate_blk0 = gate_col0 // gw
    kern = functools.partial(_attn_kernel, group=group, dk=dk, dv=dv, tq=tq, tk=tk,
                             n_chunks=n_chunks, n_sub=n_sub)
    out = pl.pallas_call(
        kern,
        grid=(batch, kv_heads, seq // tb),
        in_specs=[pl.BlockSpec((None, tb, group * dk), lambda b, h, i: (b, i, h)),
                  pl.BlockSpec((None, seq, dk), lambda b, h, i: (b, 0, h)),
                  pl.BlockSpec((None, seq // tc, dv, tc), lambda b, h, i: (h, b, 0, 0)),
                  pl.BlockSpec((None, tb, gw), lambda b, h, i: (b, i, gate_blk0 + h))],
        out_specs=pl.BlockSpec((None, tb, gw), lambda b, h, i: (b, i, h)),
        out_shape=jax.ShapeDtypeStruct((batch, seq, kv_heads * gw), BF16),
        scratch_shapes=[pltpu.VMEM((2, cols, dk), BF16),
                        pltpu.VMEM((2, tk, cols), F32),
                        pltpu.VMEM((2, tk, cols), BF16),
                        pltpu.VMEM((2, 1, cols), F32), pltpu.VMEM((2, 1, cols), F32),
                        pltpu.VMEM((2, dv + SUM_ROWS, cols), F32)],
        name="attn_gqa" if group > 1 else "attn_mla",
        compiler_params=_params(),
    )(q3, k3, vt, g3)
    return out.reshape(batch * seq, kv_heads * gw)


def _out_kernel(oa_ref, ob_ref, ma_ref, mb_ref, x_ref, wa_ref, wb_ref, wo_ref, g_ref,
                *out_refs, last):
    pa = jnp.dot(oa_ref[...], wa_ref[...], preferred_element_type=F32)
    pb = jnp.dot(ob_ref[...], wb_ref[...], preferred_element_type=F32)
    merged = ma_ref[...].astype(F32) * pa + mb_ref[...].astype(F32) * pb
    y = x_ref[...] + jnp.dot(merged.astype(BF16), wo_ref[...], preferred_element_type=F32)
    if last:
        out_refs[0][...] = _rms(y, g_ref[...])
    else:
        out_refs[0][...] = y
        out_refs[1][...] = _rms(y, g_ref[...]).astype(BF16)


def _merge_out(oa, ob, gates, x, w_br_a, w_br_b, w_o, g_next, tm, last):
    t, dm = x.shape
    wa = oa.shape[1]
    mix0 = (A_WIDTH + B_WIDTH) // dm
    row = lambda i: (i, 0)
    out_specs = [pl.BlockSpec((tm, dm), row)]
    out_shape = [jax.ShapeDtypeStruct((t, dm), F32)]
    if not last:
        out_specs.append(pl.BlockSpec((tm, dm), row))
        out_shape.append(jax.ShapeDtypeStruct((t, dm), BF16))
    return pl.pallas_call(
        functools.partial(_out_kernel, last=last),
        grid=(t // tm,),
        in_specs=[pl.BlockSpec((tm, wa), row),
                  pl.BlockSpec((tm, wa), row),
                  pl.BlockSpec((tm, dm), lambda i: (i, mix0)),
                  pl.BlockSpec((tm, dm), lambda i: (i, mix0 + 1)),
                  pl.BlockSpec((tm, dm), row),
                  _resident(w_br_a.shape), _resident(w_br_b.shape), _resident(w_o.shape),
                  _resident((1, dm))],
        out_specs=out_specs,
        out_shape=out_shape,
        name="merge_out_final" if last else "merge_out",
        compiler_params=_params(),
    )(oa, ob, gates, gates, x, w_br_a, w_br_b, w_o, g_next)


def _axial_tables(n_tokens, rot_dim):
    rows = n_tokens // GRID_W
    row = jnp.repeat(jnp.arange(rows, dtype=F32), GRID_W)
    col = (jnp.arange(rows * GRID_W) % GRID_W).astype(F32)
    n_freq = rot_dim // 4
    inv = ROPE_THETA ** (-jnp.arange(n_freq, dtype=F32) / n_freq)
    ang = jnp.concatenate([row[:, None] * inv, col[:, None] * inv], axis=-1)
    return jnp.cos(ang), jnp.sin(ang)


def _rope_tables(seq):
    ca, sa = _axial_tables(seq, A_HEAD_DIM)
    c_a = jnp.concatenate([ca, ca], axis=-1)
    s_a = jnp.concatenate([-sa, sa], axis=-1)
    sc_a = A_HEAD_DIM ** -0.5 * LOG2_E
    tabs_a = (c_a * sc_a, s_a * sc_a, c_a, s_a)
    cb, sb = _axial_tables(seq, B_ROPE)
    z32 = jnp.zeros_like(cb)
    c_b = jnp.concatenate([cb, cb, z32, z32], axis=-1)
    s1_b = jnp.concatenate([-sb, z32, z32, z32], axis=-1)
    s2_b = jnp.concatenate([z32, sb, z32, z32], axis=-1)
    sc_b = B_QK ** -0.5 * LOG2_E
    tabs_b = (c_b * sc_b, s1_b * sc_b, s2_b * sc_b, c_b, s1_b, s2_b)
    return tabs_a, tabs_b


def _split_w_in(w_in):
    depth, dm, _ = w_in.shape
    kvw = A_KV_HEADS * A_HEAD_DIM
    sizes = (A_WIDTH, kvw, kvw, A_WIDTH, B_HEADS * B_QK, B_KV_RANK, B_ROPE, B_WIDTH, dm, dm)
    parts = []
    off = 0
    for s in sizes:
        parts.append(w_in[:, :, off:off + s])
        off += s
    qa, ka, va, gate_a, qb, ckv, krope, gate_b, mix_a, mix_b = parts
    w_a = jnp.concatenate([qa, ka, va], axis=-1).astype(BF16)
    qb = qb.reshape(depth, dm, B_HEADS, B_QK)
    qb_nope = qb[..., :B_NOPE].reshape(depth, dm, B_HEADS * B_NOPE)
    qb_rope = qb[..., B_NOPE:].reshape(depth, dm, B_HEADS * B_ROPE)
    krope = jnp.pad(krope, ((0, 0), (0, 0), (0, LANES - B_ROPE)))
    w_b = jnp.concatenate([qb_nope, qb_rope, ckv, krope], axis=-1).astype(BF16)
    w_g = jnp.concatenate([gate_a, gate_b, mix_a, mix_b], axis=-1).astype(BF16)
    return w_a, w_b, w_g


def _tiles(seq):
    tm = min(512, seq // 2)
    tk = min(TK, seq // 2)
    tq_a = min(TQ_A, seq)
    tq_b = min(TQ_B, seq)
    n_sub_a = min(N_SUB, seq // tq_a)
    n_sub_b = min(N_SUB, seq // tq_b)
    return tm, tk, tq_a, tq_b, n_sub_a, n_sub_b


def _trunk(x, weights, final_g):
    batch, seq, dm = x.shape
    (norm_g, w_a, w_b, w_g, q_norm_g, k_norm_g, kv_norm_g, w_ukv, w_branch, w_out) = weights
    depth = norm_g.shape[0]
    tm, tk, tq_a, tq_b, n_sub_a, n_sub_b = _tiles(seq)
    tabs_a, tabs_b = _rope_tables(seq)
    xt = x.reshape(batch * seq, dm)
    h = _input_norm(xt, norm_g[0][None, :], tm)
    for l in range(depth):
        q_a, k_a, vt_a = _proj_a(h, w_a[l], q_norm_g[l][None, :], k_norm_g[l][None, :],
                                 tabs_a, seq, tm)
        q_b, k_b, vt_b = _proj_b(h, w_b[l], kv_norm_g[l][None, :], w_ukv[l], tabs_b, seq, tm)
        gates = _gates(h, w_g[l], tm, dm)
        oa = _attention(q_a, k_a, vt_a, gates, batch=batch, seq=seq, kv_heads=A_KV_HEADS,
                        group=A_GROUP, dk=A_HEAD_DIM, dv=A_HEAD_DIM, tq=tq_a, tk=tk,
                        n_sub=n_sub_a, gate_col0=0)
        ob = _attention(q_b, k_b, vt_b, gates, batch=batch, seq=seq, kv_heads=B_HEADS,
                        group=1, dk=B_QK_PAD, dv=B_V, tq=tq_b, tk=tk, n_sub=n_sub_b,
                        gate_col0=A_WIDTH)
        last = l == depth - 1
        g_next = final_g if last else norm_g[l + 1]
        res = _merge_out(oa, ob, gates, xt, w_branch[l, 0], w_branch[l, 1], w_out[l],
                         g_next[None, :], min(256, tm), last)
        if last:
            xt = res[0]
        else:
            xt, h = res
    return xt.reshape(batch, seq, dm)


def kernel(x_prompt, x_sample, norm_g, w_in, q_norm_g, k_norm_g, kv_norm_g, w_ukv, w_branch,
           w_out, final_norm_g):
    w_a, w_b, w_g = _split_w_in(w_in)
    weights = (norm_g, w_a, w_b, w_g, q_norm_g, k_norm_g, kv_norm_g, w_ukv.astype(BF16),
               w_branch.astype(BF16), w_out.astype(BF16))
    y_prompt = _trunk(x_prompt, weights, final_norm_g)
    y_sample = _trunk(x_sample, weights, final_norm_g)
    return (y_prompt, y_sample)
```

```python
import functools

import jax
import jax.numpy as jnp
from jax import lax
from jax.experimental import pallas as pl
from jax.experimental.pallas import tpu as pltpu

F32 = jnp.float32
BF16 = jnp.bfloat16

GRID_W = 64
ROPE_THETA = 10000.0
EPS = 1e-6
A_HEADS = 8
A_KV_HEADS = 2
A_HEAD_DIM = 128
A_GROUP = A_HEADS // A_KV_HEADS
A_WIDTH = A_HEADS * A_HEAD_DIM
B_HEADS = 8
B_NOPE = 128
B_ROPE = 64
B_QK = B_NOPE + B_ROPE
B_QK_PAD = 256
B_V = 128
B_KV_RANK = 512
B_WIDTH = B_HEADS * B_V
LANES = 128
LOG2_E = 1.4426950408889634

VMEM_LIMIT = 56 * 1024 * 1024

TK = 512
TQ_A = 256
TQ_B = 1024
N_SUB_A = 4
N_SUB_B = 2
ROW_SPLIT_A = 4
ROW_SPLIT_B = 2
ROW_SPLIT_G = 2
ROW_SPLIT_O = 2
TM_GATES = 1024
TM_OUT = 512
SUM_ROWS = 16


def _params(**kw):
    return pltpu.CompilerParams(vmem_limit_bytes=VMEM_LIMIT, **kw)


def _resident(shape):
    nd = len(shape)
    return pl.BlockSpec(shape, lambda *_: (0,) * nd, pipeline_mode=pl.Buffered(1))


def _rms(z, g):
    return z * lax.rsqrt(jnp.mean(z * z, axis=-1, keepdims=True) + EPS) * g


def _norm_kernel(x_ref, g_ref, h_ref):
    h_ref[...] = _rms(x_ref[...], g_ref[...]).astype(BF16)


def _input_norm(x, g, tm):
    t, d = x.shape
    return pl.pallas_call(
        _norm_kernel,
        grid=(t // tm,),
        in_specs=[pl.BlockSpec((tm, d), lambda i: (i, 0)),
                  pl.BlockSpec((1, d), lambda i: (0, 0))],
        out_specs=pl.BlockSpec((tm, d), lambda i: (i, 0)),
        out_shape=jax.ShapeDtypeStruct((t, d), BF16),
        name="input_norm",
        compiler_params=_params(),
    )(x, g)


def _proj_a_kernel(h_ref, w_ref, qg_ref, kg_ref, cq_ref, sq_ref, ck_ref, sk_ref,
                   q_ref, k_ref, vt_ref):
    d = A_HEAD_DIM
    k0 = A_WIDTH
    v0 = A_WIDTH + A_KV_HEADS * d
    tr = h_ref.shape[0] // ROW_SPLIT_A
    for r in range(ROW_SPLIT_A):
        rows = slice(r * tr, (r + 1) * tr)
        z = jnp.dot(h_ref[rows, :], w_ref[...], preferred_element_type=F32)
        for hd in range(A_HEADS):
            y = _rms(z[:, hd * d:(hd + 1) * d], qg_ref[...])
            y = y * cq_ref[rows, :] + pltpu.roll(y, d // 2, 1) * sq_ref[rows, :]
            q_ref[rows, hd * d:(hd + 1) * d] = y.astype(BF16)
        for hd in range(A_KV_HEADS):
            y = _rms(z[:, k0 + hd * d:k0 + (hd + 1) * d], kg_ref[...])
            y = y * ck_ref[rows, :] + pltpu.roll(y, d // 2, 1) * sk_ref[rows, :]
            k_ref[rows, hd * d:(hd + 1) * d] = y.astype(BF16)
            vt_ref[hd, 0, :, rows] = z[:, v0 + hd * d:v0 + (hd + 1) * d].T.astype(BF16)


def _proj_a(h, w_a, qg, kg, tabs, seq, tm):
    t, dm = h.shape
    nw = w_a.shape[1]
    npos = seq // tm
    tab = pl.BlockSpec((tm, LANES), lambda i: (i % npos, 0))
    return pl.pallas_call(
        _proj_a_kernel,
        grid=(t // tm,),
        in_specs=[pl.BlockSpec((tm, dm), lambda i: (i, 0)),
                  _resident((dm, nw)),
                  _resident((1, LANES)), _resident((1, LANES)),
                  tab, tab, tab, tab],
        out_specs=[pl.BlockSpec((tm, A_WIDTH), lambda i: (i, 0)),
                   pl.BlockSpec((tm, A_KV_HEADS * A_HEAD_DIM), lambda i: (i, 0)),
                   pl.BlockSpec((A_KV_HEADS, 1, A_HEAD_DIM, tm), lambda i: (0, i, 0, 0))],
        out_shape=[jax.ShapeDtypeStruct((t, A_WIDTH), BF16),
                   jax.ShapeDtypeStruct((t, A_KV_HEADS * A_HEAD_DIM), BF16),
                   jax.ShapeDtypeStruct((A_KV_HEADS, t // tm, A_HEAD_DIM, tm), BF16)],
        name="proj_a",
        compiler_params=_params(),
    )(h, w_a, qg, kg, *tabs)


def _rope_b(y, c, s1, s2):
    return (y * c + pltpu.roll(y, LANES - B_ROPE // 2, 1) * s1
            + pltpu.roll(y, B_ROPE // 2, 1) * s2)


def _proj_b_kernel(h_ref, w_ref, kvg_ref, wukv_ref, cq_ref, s1q_ref, s2q_ref,
                   ck_ref, s1k_ref, s2k_ref, q_ref, k_ref, vt_ref):
    scale = B_QK ** -0.5 * LOG2_E
    p = B_QK_PAD
    r0 = B_HEADS * B_NOPE
    c0 = r0 + B_HEADS * B_ROPE
    tr = h_ref.shape[0] // ROW_SPLIT_B
    for r in range(ROW_SPLIT_B):
        rows = slice(r * tr, (r + 1) * tr)
        z = jnp.dot(h_ref[rows, :], w_ref[...], preferred_element_type=F32)
        c = _rms(z[:, c0:c0 + B_KV_RANK], kvg_ref[...]).astype(BF16)
        kv = jnp.dot(c, wukv_ref[...], preferred_element_type=F32)
        kr = _rope_b(z[:, c0 + B_KV_RANK:c0 + B_KV_RANK + LANES],
                     ck_ref[rows, :], s1k_ref[rows, :], s2k_ref[rows, :]).astype(BF16)
        for hd in range(B_HEADS):
            q_ref[rows, hd * p:hd * p + B_NOPE] = (
                z[:, hd * B_NOPE:(hd + 1) * B_NOPE] * scale).astype(BF16)
            pair = z[:, r0 + (hd // 2) * LANES:r0 + (hd // 2 + 1) * LANES]
            if hd % 2:
                pair = pltpu.roll(pair, LANES // 2, 1)
            qr = _rope_b(pair, cq_ref[rows, :], s1q_ref[rows, :], s2q_ref[rows, :])
            q_ref[rows, hd * p + B_NOPE:(hd + 1) * p] = qr.astype(BF16)
            k_ref[rows, hd * p:hd * p + B_NOPE] = kv[:, hd * p:hd * p + B_NOPE].astype(BF16)
            k_ref[rows, hd * p + B_NOPE:(hd + 1) * p] = kr
            vt_ref[hd, 0, :, rows] = kv[:, hd * p + B_NOPE:(hd + 1) * p].T.astype(BF16)


def _proj_b(h, w_b, kvg, w_ukv, tabs, seq, tm):
    t, dm = h.shape
    nw = w_b.shape[1]
    npos = seq // tm
    tab = pl.BlockSpec((tm, LANES), lambda i: (i % npos, 0))
    wide = B_HEADS * B_QK_PAD
    return pl.pallas_call(
        _proj_b_kernel,
        grid=(t // tm,),
        in_specs=[pl.BlockSpec((tm, dm), lambda i: (i, 0)),
                  _resident((dm, nw)),
                  _resident((1, B_KV_RANK)),
                  _resident(w_ukv.shape),
                  tab, tab, tab, tab, tab, tab],
        out_specs=[pl.BlockSpec((tm, wide), lambda i: (i, 0)),
                   pl.BlockSpec((tm, wide), lambda i: (i, 0)),
                   pl.BlockSpec((B_HEADS, 1, B_V, tm), lambda i: (0, i, 0, 0))],
        out_shape=[jax.ShapeDtypeStruct((t, wide), BF16),
                   jax.ShapeDtypeStruct((t, wide), BF16),
                   jax.ShapeDtypeStruct((B_HEADS, t // tm, B_V, tm), BF16)],
        name="proj_b",
        compiler_params=_params(),
    )(h, w_b, kvg, w_ukv, *tabs)


def _gates_kernel(h_ref, w_ref, o_ref):
    tr = h_ref.shape[0] // ROW_SPLIT_G
    for r in range(ROW_SPLIT_G):
        rows = slice(r * tr, (r + 1) * tr)
        z = jnp.dot(h_ref[rows, :], w_ref[...], preferred_element_type=F32)
        sig = 1.0 / (1.0 + jnp.exp(-z))
        lin = jnp.where(pl.program_id(0) == 0, z, jnp.ones_like(z))
        o_ref[rows, :] = (sig * lin).astype(BF16)


def _gates(h, w_g, tm, tn):
    t, dm = h.shape
    n = w_g.shape[1]
    return pl.pallas_call(
        _gates_kernel,
        grid=(n // tn, t // tm),
        in_specs=[pl.BlockSpec((tm, dm), lambda j, i: (i, 0)),
                  pl.BlockSpec((dm, tn), lambda j, i: (0, j))],
        out_specs=pl.BlockSpec((tm, tn), lambda j, i: (i, j)),
        out_shape=jax.ShapeDtypeStruct((t, n), BF16),
        name="gates",
        compiler_params=_params(),
    )(h, w_g)


def _attn_kernel(q_ref, k_ref, vt_ref, g_ref, o_ref, qs_ref, s_ref, p_ref,
                 m_ref, a_ref, acc_ref, *, group, dk, dv, tq, tk, n_chunks, n_sub):
    tc = vt_ref.shape[-1]
    w = min(tk, tc)
    items = [(sb, j) for sb in range(n_sub) for j in range(n_chunks)]

    def scores(t):
        sb, j = items[t]
        qs = qs_ref.at[sb % 2]
        if j == 0:
            r0 = sb * tq
            for g in range(group):
                qs[g * tq:(g + 1) * tq, :] = q_ref[r0:r0 + tq, g * dk:(g + 1) * dk]
        s_ref[t % 2] = lax.dot_general(k_ref[j * tk:(j + 1) * tk, :], qs[...],
                                       (((1,), (1,)), ((), ())),
                                       preferred_element_type=F32)

    def softmax(t):
        sb, j = items[t]
        st = sb % 2
        s = s_ref[t % 2]
        if j == 0:
            m_new = jnp.max(s, axis=0, keepdims=True)
            alpha = jnp.zeros_like(m_new)
        else:
            m_old = m_ref[st]
            m_new = jnp.maximum(m_old, jnp.max(s, axis=0, keepdims=True))
            alpha = jnp.exp2(m_old - m_new)
        p_ref[t % 2] = jnp.exp2(s - m_new).astype(BF16)
        m_ref[st] = m_new
        return alpha

    def values(t):
        sb, j = items[t]
        st = sb % 2
        ones = jnp.ones((SUM_ROWS, w), BF16)
        pv = None
        for off in range(0, tk, w):
            row = j * tk + off
            vt1 = jnp.concatenate([vt_ref[row // tc, :, row % tc:row % tc + w], ones], axis=0)
            part = jnp.dot(vt1, p_ref[t % 2, off:off + w, :],
                           preferred_element_type=F32)
            pv = part if pv is None else pv + part
        if j == 0:
            acc_ref[st] = pv
        else:
            acc_ref[st] = a_ref[st] * acc_ref[st] + pv
        if j == n_chunks - 1:
            r0 = sb * tq
            o = acc_ref[st, 0:dv, :] * (1.0 / acc_ref[st, dv:dv + 1, :])
            for g in range(group):
                og = o[:, g * tq:(g + 1) * tq].T
                gate = g_ref[r0:r0 + tq, g * dv:(g + 1) * dv].astype(F32)
                o_ref[r0:r0 + tq, g * dv:(g + 1) * dv] = (og * gate).astype(BF16)

    scores(0)
    for t in range(len(items)):
        if t + 1 < len(items):
            scores(t + 1)
        alpha = softmax(t)
        if t > 0:
            values(t - 1)
        a_ref[items[t][0] % 2] = alpha
    values(len(items) - 1)


def _attention(q, k, vt, gates, *, batch, seq, kv_heads, group, dk, dv, tq, tk, n_sub, gate_col0):
    tc = vt.shape[-1]
    n_chunks = seq // tk
    assert seq % tk == 0 and n_chunks >= 2 and (tk % tc == 0 or tc % tk == 0), (seq, tk, tc)
    cols = group * tq
    tb = n_sub * tq
    q3 = q.reshape(batch, seq, q.shape[1])
    k3 = k.reshape(batch, seq, k.shape[1])
    g3 = gates.reshape(batch, seq, gates.shape[1])
    gw = group * dv
    gate_blk0 = gate_col0 // gw
    kern = functools.partial(_attn_kernel, group=group, dk=dk, dv=dv, tq=tq, tk=tk,
                             n_chunks=n_chunks, n_sub=n_sub)
    out = pl.pallas_call(
        kern,
        grid=(batch, kv_heads, seq // tb),
        in_specs=[pl.BlockSpec((None, tb, group * dk), lambda b, h, i: (b, i, h)),
                  pl.BlockSpec((None, seq, dk), lambda b, h, i: (b, 0, h)),
                  pl.BlockSpec((None, seq // tc, dv, tc), lambda b, h, i: (h, b, 0, 0)),
                  pl.BlockSpec((None, tb, gw), lambda b, h, i: (b, i, gate_blk0 + h))],
        out_specs=pl.BlockSpec((None, tb, gw), lambda b, h, i: (b, i, h)),
        out_shape=jax.ShapeDtypeStruct((batch, seq, kv_heads * gw), BF16),
        scratch_shapes=[pltpu.VMEM((2, cols, dk), BF16),
                        pltpu.VMEM((2, tk, cols), F32),
                        pltpu.VMEM((2, tk, cols), BF16),
                        pltpu.VMEM((2, 1, cols), F32), pltpu.VMEM((2, 1, cols), F32),
                        pltpu.VMEM((2, dv + SUM_ROWS, cols), F32)],
        name="attn_gqa" if group > 1 else "attn_mla",
        compiler_params=_params(),
    )(q3, k3, vt, g3)
    return out.reshape(batch * seq, kv_heads * gw)


def _out_kernel(oa_ref, ob_ref, ma_ref, mb_ref, x_ref, wa_ref, wb_ref, wo_ref, g_ref,
                *out_refs, last):
    tr = x_ref.shape[0] // ROW_SPLIT_O
    for r in range(ROW_SPLIT_O):
        rows = slice(r * tr, (r + 1) * tr)
        pa = jnp.dot(oa_ref[rows, :], wa_ref[...], preferred_element_type=F32)
        pb = jnp.dot(ob_ref[rows, :], wb_ref[...], preferred_element_type=F32)
        merged = ma_ref[rows, :].astype(F32) * pa + mb_ref[rows, :].astype(F32) * pb
        y = x_ref[rows, :] + jnp.dot(merged.astype(BF16), wo_ref[...],
                                     preferred_element_type=F32)
        if last:
            out_refs[0][rows, :] = _rms(y, g_ref[...])
        else:
            out_refs[0][rows, :] = y
            out_refs[1][rows, :] = _rms(y, g_ref[...]).astype(BF16)


def _merge_out(oa, ob, gates, x, w_br_a, w_br_b, w_o, g_next, tm, last):
    t, dm = x.shape
    wa = oa.shape[1]
    mix0 = (A_WIDTH + B_WIDTH) // dm
    row = lambda i: (i, 0)
    out_specs = [pl.BlockSpec((tm, dm), row)]
    out_shape = [jax.ShapeDtypeStruct((t, dm), F32)]
    if not last:
        out_specs.append(pl.BlockSpec((tm, dm), row))
        out_shape.append(jax.ShapeDtypeStruct((t, dm), BF16))
    return pl.pallas_call(
        functools.partial(_out_kernel, last=last),
        grid=(t // tm,),
        in_specs=[pl.BlockSpec((tm, wa), row),
                  pl.BlockSpec((tm, wa), row),
                  pl.BlockSpec((tm, dm), lambda i: (i, mix0)),
                  pl.BlockSpec((tm, dm), lambda i: (i, mix0 + 1)),
                  pl.BlockSpec((tm, dm), row),
                  _resident(w_br_a.shape), _resident(w_br_b.shape), _resident(w_o.shape),
                  _resident((1, dm))],
        out_specs=out_specs,
        out_shape=out_shape,
        name="merge_out_final" if last else "merge_out",
        compiler_params=_params(),
    )(oa, ob, gates, gates, x, w_br_a, w_br_b, w_o, g_next)


def _axial_tables(n_tokens, rot_dim):
    rows = n_tokens // GRID_W
    row = jnp.repeat(jnp.arange(rows, dtype=F32), GRID_W)
    col = (jnp.arange(rows * GRID_W) % GRID_W).astype(F32)
    n_freq = rot_dim // 4
    inv = ROPE_THETA ** (-jnp.arange(n_freq, dtype=F32) / n_freq)
    ang = jnp.concatenate([row[:, None] * inv, col[:, None] * inv], axis=-1)
    return jnp.cos(ang), jnp.sin(ang)


def _rope_tables(seq):
    ca, sa = _axial_tables(seq, A_HEAD_DIM)
    c_a = jnp.concatenate([ca, ca], axis=-1)
    s_a = jnp.concatenate([-sa, sa], axis=-1)
    sc_a = A_HEAD_DIM ** -0.5 * LOG2_E
    tabs_a = (c_a * sc_a, s_a * sc_a, c_a, s_a)
    cb, sb = _axial_tables(seq, B_ROPE)
    z32 = jnp.zeros_like(cb)
    c_b = jnp.concatenate([cb, cb, z32, z32], axis=-1)
    s1_b = jnp.concatenate([-sb, z32, z32, z32], axis=-1)
    s2_b = jnp.concatenate([z32, sb, z32, z32], axis=-1)
    sc_b = B_QK ** -0.5 * LOG2_E
    tabs_b = (c_b * sc_b, s1_b * sc_b, s2_b * sc_b, c_b, s1_b, s2_b)
    return tabs_a, tabs_b


def _split_w_in(w_in):
    depth, dm, _ = w_in.shape
    kvw = A_KV_HEADS * A_HEAD_DIM
    sizes = (A_WIDTH, kvw, kvw, A_WIDTH, B_HEADS * B_QK, B_KV_RANK, B_ROPE, B_WIDTH, dm, dm)
    parts = []
    off = 0
    for s in sizes:
        parts.append(w_in[:, :, off:off + s])
        off += s
    qa, ka, va, gate_a, qb, ckv, krope, gate_b, mix_a, mix_b = parts
    w_a = jnp.concatenate([qa, ka, va], axis=-1).astype(BF16)
    qb = qb.reshape(depth, dm, B_HEADS, B_QK)
    qb_nope = qb[..., :B_NOPE].reshape(depth, dm, B_HEADS * B_NOPE)
    qb_rope = qb[..., B_NOPE:].reshape(depth, dm, B_HEADS * B_ROPE)
    krope = jnp.pad(krope, ((0, 0), (0, 0), (0, LANES - B_ROPE)))
    w_b = jnp.concatenate([qb_nope, qb_rope, ckv, krope], axis=-1).astype(BF16)
    w_g = jnp.concatenate([gate_a, gate_b, mix_a, mix_b], axis=-1).astype(BF16)
    return w_a, w_b, w_g


def _tiles(seq):
    tm = min(512, seq // 2)
    tk = min(TK, seq // 2)
    tq_a = min(TQ_A, seq)
    tq_b = min(TQ_B, seq)
    n_sub_a = min(N_SUB_A, seq // tq_a)
    n_sub_b = min(N_SUB_B, seq // tq_b)
    return tm, tk, tq_a, tq_b, n_sub_a, n_sub_b


def _trunk(x, weights, final_g):
    batch, seq, dm = x.shape
    (norm_g, w_a, w_b, w_g, q_norm_g, k_norm_g, kv_norm_g, w_ukv, w_branch, w_out) = weights
    depth = norm_g.shape[0]
    tm, tk, tq_a, tq_b, n_sub_a, n_sub_b = _tiles(seq)
    tabs_a, tabs_b = _rope_tables(seq)
    xt = x.reshape(batch * seq, dm)
    h = _input_norm(xt, norm_g[0][None, :], tm)
    for l in range(depth):
        q_a, k_a, vt_a = _proj_a(h, w_a[l], q_norm_g[l][None, :], k_norm_g[l][None, :],
                                 tabs_a, seq, tm)
        q_b, k_b, vt_b = _proj_b(h, w_b[l], kv_norm_g[l][None, :], w_ukv[l], tabs_b, seq, tm)
        gates = _gates(h, w_g[l], min(TM_GATES, 2 * tm), dm)
        oa = _attention(q_a, k_a, vt_a, gates, batch=batch, seq=seq, kv_heads=A_KV_HEADS,
                        group=A_GROUP, dk=A_HEAD_DIM, dv=A_HEAD_DIM, tq=tq_a, tk=tk,
                        n_sub=n_sub_a, gate_col0=0)
        ob = _attention(q_b, k_b, vt_b, gates, batch=batch, seq=seq, kv_heads=B_HEADS,
                        group=1, dk=B_QK_PAD, dv=B_V, tq=tq_b, tk=tk, n_sub=n_sub_b,
                        gate_col0=A_WIDTH)
        last = l == depth - 1
        g_next = final_g if last else norm_g[l + 1]
        res = _merge_out(oa, ob, gates, xt, w_branch[l, 0], w_branch[l, 1], w_out[l],
                         g_next[None, :], min(TM_OUT, tm), last)
        if last:
            xt = res[0]
        else:
            xt, h = res
    return xt.reshape(batch, seq, dm)


def kernel(x_prompt, x_sample, norm_g, w_in, q_norm_g, k_norm_g, kv_norm_g, w_ukv, w_branch,
           w_out, final_norm_g):
    w_a, w_b, w_g = _split_w_in(w_in)
    weights = (norm_g, w_a, w_b, w_g, q_norm_g, k_norm_g, kv_norm_g, w_ukv.astype(BF16),
               w_branch.astype(BF16), w_out.astype(BF16))
    y_prompt = _trunk(x_prompt, weights, final_norm_g)
    y_sample = _trunk(x_sample, weights, final_norm_g)
    return (y_prompt, y_sample)
```

```python
import functools

import jax
import jax.numpy as jnp
from jax import lax
from jax.experimental import pallas as pl
from jax.experimental.pallas import tpu as pltpu

F32 = jnp.float32
BF16 = jnp.bfloat16

GRID_W = 64
ROPE_THETA = 10000.0
EPS = 1e-6
A_HEADS = 8
A_KV_HEADS = 2
A_HEAD_DIM = 128
A_GROUP = A_HEADS // A_KV_HEADS
A_WIDTH = A_HEADS * A_HEAD_DIM
B_HEADS = 8
B_NOPE = 128
B_ROPE = 64
B_QK = B_NOPE + B_ROPE
B_QK_PAD = 256
B_V = 128
B_KV_RANK = 512
B_WIDTH = B_HEADS * B_V
LANES = 128
LOG2_E = 1.4426950408889634

VMEM_LIMIT = 56 * 1024 * 1024

TK = 512
TQ_A = 256
TQ_B = 1024
N_SUB_A = 4
N_SUB_B = 2
MLA_ITEMS_PER_STEP = 32
ROW_SPLIT_A = 4
ROW_SPLIT_B = 2
ROW_SPLIT_G = 2
ROW_SPLIT_O = 2
TM_GATES = 1024
TM_OUT = 512
SUM_ROWS = 16


def _params(**kw):
    return pltpu.CompilerParams(vmem_limit_bytes=VMEM_LIMIT, **kw)


def _resident(shape):
    nd = len(shape)
    return pl.BlockSpec(shape, lambda *_: (0,) * nd, pipeline_mode=pl.Buffered(1))


def _rms(z, g):
    return z * lax.rsqrt(jnp.mean(z * z, axis=-1, keepdims=True) + EPS) * g


def _norm_kernel(x_ref, g_ref, h_ref):
    h_ref[...] = _rms(x_ref[...], g_ref[...]).astype(BF16)


def _input_norm(x, g, tm):
    t, d = x.shape
    return pl.pallas_call(
        _norm_kernel,
        grid=(t // tm,),
        in_specs=[pl.BlockSpec((tm, d), lambda i: (i, 0)),
                  pl.BlockSpec((1, d), lambda i: (0, 0))],
        out_specs=pl.BlockSpec((tm, d), lambda i: (i, 0)),
        out_shape=jax.ShapeDtypeStruct((t, d), BF16),
        name="input_norm",
        compiler_params=_params(),
    )(x, g)


def _proj_a_kernel(h_ref, w_ref, qg_ref, kg_ref, cq_ref, sq_ref, ck_ref, sk_ref,
                   q_ref, k_ref, vt_ref):
    d = A_HEAD_DIM
    k0 = A_WIDTH
    v0 = A_WIDTH + A_KV_HEADS * d
    tr = h_ref.shape[0] // ROW_SPLIT_A
    for r in range(ROW_SPLIT_A):
        rows = slice(r * tr, (r + 1) * tr)
        z = jnp.dot(h_ref[rows, :], w_ref[...], preferred_element_type=F32)
        for hd in range(A_HEADS):
            y = _rms(z[:, hd * d:(hd + 1) * d], qg_ref[...])
            y = y * cq_ref[rows, :] + pltpu.roll(y, d // 2, 1) * sq_ref[rows, :]
            q_ref[rows, hd * d:(hd + 1) * d] = y.astype(BF16)
        for hd in range(A_KV_HEADS):
            y = _rms(z[:, k0 + hd * d:k0 + (hd + 1) * d], kg_ref[...])
            y = y * ck_ref[rows, :] + pltpu.roll(y, d // 2, 1) * sk_ref[rows, :]
            k_ref[rows, hd * d:(hd + 1) * d] = y.astype(BF16)
            vt_ref[hd, 0, :, rows] = z[:, v0 + hd * d:v0 + (hd + 1) * d].T.astype(BF16)


def _proj_a(h, w_a, qg, kg, tabs, seq, tm):
    t, dm = h.shape
    nw = w_a.shape[1]
    npos = seq // tm
    tab = pl.BlockSpec((tm, LANES), lambda i: (i % npos, 0))
    return pl.pallas_call(
        _proj_a_kernel,
        grid=(t // tm,),
        in_specs=[pl.BlockSpec((tm, dm), lambda i: (i, 0)),
                  _resident((dm, nw)),
                  _resident((1, LANES)), _resident((1, LANES)),
                  tab, tab, tab, tab],
        out_specs=[pl.BlockSpec((tm, A_WIDTH), lambda i: (i, 0)),
                   pl.BlockSpec((tm, A_KV_HEADS * A_HEAD_DIM), lambda i: (i, 0)),
                   pl.BlockSpec((A_KV_HEADS, 1, A_HEAD_DIM, tm), lambda i: (0, i, 0, 0))],
        out_shape=[jax.ShapeDtypeStruct((t, A_WIDTH), BF16),
                   jax.ShapeDtypeStruct((t, A_KV_HEADS * A_HEAD_DIM), BF16),
                   jax.ShapeDtypeStruct((A_KV_HEADS, t // tm, A_HEAD_DIM, tm), BF16)],
        name="proj_a",
        compiler_params=_params(),
    )(h, w_a, qg, kg, *tabs)


def _rope_b(y, c, s1, s2):
    return (y * c + pltpu.roll(y, LANES - B_ROPE // 2, 1) * s1
            + pltpu.roll(y, B_ROPE // 2, 1) * s2)


def _proj_b_kernel(h_ref, w_ref, kvg_ref, wukv_ref, cq_ref, s1q_ref, s2q_ref,
                   ck_ref, s1k_ref, s2k_ref, q_ref, k_ref, vt_ref):
    scale = B_QK ** -0.5 * LOG2_E
    p = B_QK_PAD
    r0 = B_HEADS * B_NOPE
    c0 = r0 + B_HEADS * B_ROPE
    tr = h_ref.shape[0] // ROW_SPLIT_B
    for r in range(ROW_SPLIT_B):
        rows = slice(r * tr, (r + 1) * tr)
        z = jnp.dot(h_ref[rows, :], w_ref[...], preferred_element_type=F32)
        c = _rms(z[:, c0:c0 + B_KV_RANK], kvg_ref[...]).astype(BF16)
        kv = jnp.dot(c, wukv_ref[...], preferred_element_type=F32)
        kr = _rope_b(z[:, c0 + B_KV_RANK:c0 + B_KV_RANK + LANES],
                     ck_ref[rows, :], s1k_ref[rows, :], s2k_ref[rows, :]).astype(BF16)
        for hd in range(B_HEADS):
            q_ref[rows, hd * p:hd * p + B_NOPE] = (
                z[:, hd * B_NOPE:(hd + 1) * B_NOPE] * scale).astype(BF16)
            pair = z[:, r0 + (hd // 2) * LANES:r0 + (hd // 2 + 1) * LANES]
            if hd % 2:
                pair = pltpu.roll(pair, LANES // 2, 1)
            qr = _rope_b(pair, cq_ref[rows, :], s1q_ref[rows, :], s2q_ref[rows, :])
            q_ref[rows, hd * p + B_NOPE:(hd + 1) * p] = qr.astype(BF16)
            k_ref[rows, hd * p:hd * p + B_NOPE] = kv[:, hd * p:hd * p + B_NOPE].astype(BF16)
            k_ref[rows, hd * p + B_NOPE:(hd + 1) * p] = kr
            vt_ref[hd, 0, :, rows] = kv[:, hd * p + B_NOPE:(hd + 1) * p].T.astype(BF16)


def _proj_b(h, w_b, kvg, w_ukv, tabs, seq, tm):
    t, dm = h.shape
    nw = w_b.shape[1]
    npos = seq // tm
    tab = pl.BlockSpec((tm, LANES), lambda i: (i % npos, 0))
    wide = B_HEADS * B_QK_PAD
    return pl.pallas_call(
        _proj_b_kernel,
        grid=(t // tm,),
        in_specs=[pl.BlockSpec((tm, dm), lambda i: (i, 0)),
                  _resident((dm, nw)),
                  _resident((1, B_KV_RANK)),
                  _resident(w_ukv.shape),
                  tab, tab, tab, tab, tab, tab],
        out_specs=[pl.BlockSpec((tm, wide), lambda i: (i, 0)),
                   pl.BlockSpec((tm, wide), lambda i: (i, 0)),
                   pl.BlockSpec((B_HEADS, 1, B_V, tm), lambda i: (0, i, 0, 0))],
        out_shape=[jax.ShapeDtypeStruct((t, wide), BF16),
                   jax.ShapeDtypeStruct((t, wide), BF16),
                   jax.ShapeDtypeStruct((B_HEADS, t // tm, B_V, tm), BF16)],
        name="proj_b",
        compiler_params=_params(),
    )(h, w_b, kvg, w_ukv, *tabs)


def _gates_kernel(h_ref, w_ref, o_ref):
    tr = h_ref.shape[0] // ROW_SPLIT_G
    for r in range(ROW_SPLIT_G):
        rows = slice(r * tr, (r + 1) * tr)
        z = jnp.dot(h_ref[rows, :], w_ref[...], preferred_element_type=F32)
        sig = 1.0 / (1.0 + jnp.exp(-z))
        lin = jnp.where(pl.program_id(0) == 0, z, jnp.ones_like(z))
        o_ref[rows, :] = (sig * lin).astype(BF16)


def _gates(h, w_g, tm, tn):
    t, dm = h.shape
    n = w_g.shape[1]
    return pl.pallas_call(
        _gates_kernel,
        grid=(n // tn, t // tm),
        in_specs=[pl.BlockSpec((tm, dm), lambda j, i: (i, 0)),
                  pl.BlockSpec((dm, tn), lambda j, i: (0, j))],
        out_specs=pl.BlockSpec((tm, tn), lambda j, i: (i, j)),
        out_shape=jax.ShapeDtypeStruct((t, n), BF16),
        name="gates",
        compiler_params=_params(),
    )(h, w_g)


def _attn_kernel(q_ref, k_ref, vt_ref, g_ref, o_ref, qs_ref, s_ref, p_ref,
                 m_ref, a_ref, acc_ref, *, group, dk, dv, tq, tk, n_chunks, n_sub, hps):
    tc = vt_ref.shape[-1]
    w = min(tk, tc)
    gq, gw = group * dk, group * dv
    items = [(u, j) for u in range(hps * n_sub) for j in range(n_chunks)]

    def scores(t):
        u, j = items[t]
        hh, sb = divmod(u, n_sub)
        qs = qs_ref.at[u % 2]
        if j == 0:
            r0 = sb * tq
            for g in range(group):
                qs[g * tq:(g + 1) * tq, :] = q_ref[r0:r0 + tq,
                                                   hh * gq + g * dk:hh * gq + (g + 1) * dk]
        s_ref[t % 2] = lax.dot_general(k_ref[j * tk:(j + 1) * tk, hh * dk:(hh + 1) * dk], qs[...],
                                       (((1,), (1,)), ((), ())),
                                       preferred_element_type=F32)

    def softmax(t):
        u, j = items[t]
        st = u % 2
        s = s_ref[t % 2]
        if j == 0:
            m_new = jnp.max(s, axis=0, keepdims=True)
            alpha = jnp.zeros_like(m_new)
        else:
            m_old = m_ref[st]
            m_new = jnp.maximum(m_old, jnp.max(s, axis=0, keepdims=True))
            alpha = jnp.exp2(m_old - m_new)
        p_ref[t % 2] = jnp.exp2(s - m_new).astype(BF16)
        m_ref[st] = m_new
        return alpha

    def values(t):
        u, j = items[t]
        hh, sb = divmod(u, n_sub)
        st = u % 2
        ones = jnp.ones((SUM_ROWS, w), BF16)
        pv = None
        for off in range(0, tk, w):
            row = j * tk + off
            vt1 = jnp.concatenate([vt_ref[hh, row // tc, :, row % tc:row % tc + w], ones], axis=0)
            part = jnp.dot(vt1, p_ref[t % 2, off:off + w, :],
                           preferred_element_type=F32)
            pv = part if pv is None else pv + part
        if j == 0:
            acc_ref[st] = pv
        else:
            acc_ref[st] = a_ref[st] * acc_ref[st] + pv
        if j == n_chunks - 1:
            r0 = sb * tq
            o = acc_ref[st, 0:dv, :] * (1.0 / acc_ref[st, dv:dv + 1, :])
            for g in range(group):
                og = o[:, g * tq:(g + 1) * tq].T
                c0 = hh * gw + g * dv
                gate = g_ref[r0:r0 + tq, c0:c0 + dv].astype(F32)
                o_ref[r0:r0 + tq, c0:c0 + dv] = (og * gate).astype(BF16)

    scores(0)
    for t in range(len(items)):
        if t + 1 < len(items):
            scores(t + 1)
        alpha = softmax(t)
        if t > 0:
            values(t - 1)
        a_ref[items[t][0] % 2] = alpha
    values(len(items) - 1)


def _attention(q, k, vt, gates, *, batch, seq, kv_heads, group, dk, dv, tq, tk, n_sub, hps,
               gate_col0):
    tc = vt.shape[-1]
    n_chunks = seq // tk
    assert seq % tk == 0 and n_chunks >= 2 and (tk % tc == 0 or tc % tk == 0), (seq, tk, tc)
    cols = group * tq
    tb = n_sub * tq
    q3 = q.reshape(batch, seq, q.shape[1])
    k3 = k.reshape(batch, seq, k.shape[1])
    g3 = gates.reshape(batch, seq, gates.shape[1])
    gw = group * dv
    assert kv_heads % hps == 0 and gate_col0 % (hps * gw) == 0
    gate_blk0 = gate_col0 // (hps * gw)
    kern = functools.partial(_attn_kernel, group=group, dk=dk, dv=dv, tq=tq, tk=tk,
                             n_chunks=n_chunks, n_sub=n_sub, hps=hps)
    out = pl.pallas_call(
        kern,
        grid=(batch, kv_heads // hps, seq // tb),
        in_specs=[pl.BlockSpec((None, tb, hps * group * dk), lambda b, h, i: (b, i, h)),
                  pl.BlockSpec((None, seq, hps * dk), lambda b, h, i: (b, 0, h)),
                  pl.BlockSpec((hps, seq // tc, dv, tc), lambda b, h, i: (h, b, 0, 0)),
                  pl.BlockSpec((None, tb, hps * gw), lambda b, h, i: (b, i, gate_blk0 + h))],
        out_specs=pl.BlockSpec((None, tb, hps * gw), lambda b, h, i: (b, i, h)),
        out_shape=jax.ShapeDtypeStruct((batch, seq, kv_heads * gw), BF16),
        scratch_shapes=[pltpu.VMEM((2, cols, dk), BF16),
                        pltpu.VMEM((2, tk, cols), F32),
                        pltpu.VMEM((2, tk, cols), BF16),
                        pltpu.VMEM((2, 1, cols), F32), pltpu.VMEM((2, 1, cols), F32),
                        pltpu.VMEM((2, dv + SUM_ROWS, cols), F32)],
        name="attn_gqa" if group > 1 else "attn_mla",
        compiler_params=_params(),
    )(q3, k3, vt, g3)
    return out.reshape(batch * seq, kv_heads * gw)


def _out_kernel(oa_ref, ob_ref, ma_ref, mb_ref, x_ref, wa_ref, wb_ref, wo_ref, g_ref,
                *out_refs, last):
    tr = x_ref.shape[0] // ROW_SPLIT_O
    for r in range(ROW_SPLIT_O):
        rows = slice(r * tr, (r + 1) * tr)
        pa = jnp.dot(oa_ref[rows, :], wa_ref[...], preferred_element_type=F32)
        pb = jnp.dot(ob_ref[rows, :], wb_ref[...], preferred_element_type=F32)
        merged = ma_ref[rows, :].astype(F32) * pa + mb_ref[rows, :].astype(F32) * pb
        y = x_ref[rows, :] + jnp.dot(merged.astype(BF16), wo_ref[...],
                                     preferred_element_type=F32)
        if last:
            out_refs[0][rows, :] = _rms(y, g_ref[...])
        else:
            out_refs[0][rows, :] = y
            out_refs[1][rows, :] = _rms(y, g_ref[...]).astype(BF16)


def _merge_out(oa, ob, gates, x, w_br_a, w_br_b, w_o, g_next, tm, last):
    t, dm = x.shape
    wa = oa.shape[1]
    mix0 = (A_WIDTH + B_WIDTH) // dm
    row = lambda i: (i, 0)
    out_specs = [pl.BlockSpec((tm, dm), row)]
    out_shape = [jax.ShapeDtypeStruct((t, dm), F32)]
    if not last:
        out_specs.append(pl.BlockSpec((tm, dm), row))
        out_shape.append(jax.ShapeDtypeStruct((t, dm), BF16))
    return pl.pallas_call(
        functools.partial(_out_kernel, last=last),
        grid=(t // tm,),
        in_specs=[pl.BlockSpec((tm, wa), row),
                  pl.BlockSpec((tm, wa), row),
                  pl.BlockSpec((tm, dm), lambda i: (i, mix0)),
                  pl.BlockSpec((tm, dm), lambda i: (i, mix0 + 1)),
                  pl.BlockSpec((tm, dm), row),
                  _resident(w_br_a.shape), _resident(w_br_b.shape), _resident(w_o.shape),
                  _resident((1, dm))],
        out_specs=out_specs,
        out_shape=out_shape,
        name="merge_out_final" if last else "merge_out",
        compiler_params=_params(),
    )(oa, ob, gates, gates, x, w_br_a, w_br_b, w_o, g_next)


def _axial_tables(n_tokens, rot_dim):
    rows = n_tokens // GRID_W
    row = jnp.repeat(jnp.arange(rows, dtype=F32), GRID_W)
    col = (jnp.arange(rows * GRID_W) % GRID_W).astype(F32)
    n_freq = rot_dim // 4
    inv = ROPE_THETA ** (-jnp.arange(n_freq, dtype=F32) / n_freq)
    ang = jnp.concatenate([row[:, None] * inv, col[:, None] * inv], axis=-1)
    return jnp.cos(ang), jnp.sin(ang)


def _rope_tables(seq):
    ca, sa = _axial_tables(seq, A_HEAD_DIM)
    c_a = jnp.concatenate([ca, ca], axis=-1)
    s_a = jnp.concatenate([-sa, sa], axis=-1)
    sc_a = A_HEAD_DIM ** -0.5 * LOG2_E
    tabs_a = (c_a * sc_a, s_a * sc_a, c_a, s_a)
    cb, sb = _axial_tables(seq, B_ROPE)
    z32 = jnp.zeros_like(cb)
    c_b = jnp.concatenate([cb, cb, z32, z32], axis=-1)
    s1_b = jnp.concatenate([-sb, z32, z32, z32], axis=-1)
    s2_b = jnp.concatenate([z32, sb, z32, z32], axis=-1)
    sc_b = B_QK ** -0.5 * LOG2_E
    tabs_b = (c_b * sc_b, s1_b * sc_b, s2_b * sc_b, c_b, s1_b, s2_b)
    return tabs_a, tabs_b


def _split_w_in(w_in):
    depth, dm, _ = w_in.shape
    kvw = A_KV_HEADS * A_HEAD_DIM
    sizes = (A_WIDTH, kvw, kvw, A_WIDTH, B_HEADS * B_QK, B_KV_RANK, B_ROPE, B_WIDTH, dm, dm)
    parts = []
    off = 0
    for s in sizes:
        parts.append(w_in[:, :, off:off + s])
        off += s
    qa, ka, va, gate_a, qb, ckv, krope, gate_b, mix_a, mix_b = parts
    w_a = jnp.concatenate([qa, ka, va], axis=-1).astype(BF16)
    qb = qb.reshape(depth, dm, B_HEADS, B_QK)
    qb_nope = qb[..., :B_NOPE].reshape(depth, dm, B_HEADS * B_NOPE)
    qb_rope = qb[..., B_NOPE:].reshape(depth, dm, B_HEADS * B_ROPE)
    krope = jnp.pad(krope, ((0, 0), (0, 0), (0, LANES - B_ROPE)))
    w_b = jnp.concatenate([qb_nope, qb_rope, ckv, krope], axis=-1).astype(BF16)
    w_g = jnp.concatenate([gate_a, gate_b, mix_a, mix_b], axis=-1).astype(BF16)
    return w_a, w_b, w_g


def _tiles(seq):
    tm = min(512, seq // 2)
    tk = min(TK, seq // 2)
    tq_a = min(TQ_A, seq)
    tq_b = min(TQ_B, seq)
    n_sub_a = min(N_SUB_A, seq // tq_a)
    n_sub_b = min(N_SUB_B, seq // tq_b)
    return tm, tk, tq_a, tq_b, n_sub_a, n_sub_b


def _trunk(x, weights, final_g):
    batch, seq, dm = x.shape
    (norm_g, w_a, w_b, w_g, q_norm_g, k_norm_g, kv_norm_g, w_ukv, w_branch, w_out) = weights
    depth = norm_g.shape[0]
    tm, tk, tq_a, tq_b, n_sub_a, n_sub_b = _tiles(seq)
    hps_b = max(1, min(B_HEADS, MLA_ITEMS_PER_STEP // (n_sub_b * (seq // tk))))
    tabs_a, tabs_b = _rope_tables(seq)
    xt = x.reshape(batch * seq, dm)
    h = _input_norm(xt, norm_g[0][None, :], tm)
    for l in range(depth):
        q_a, k_a, vt_a = _proj_a(h, w_a[l], q_norm_g[l][None, :], k_norm_g[l][None, :],
                                 tabs_a, seq, tm)
        q_b, k_b, vt_b = _proj_b(h, w_b[l], kv_norm_g[l][None, :], w_ukv[l], tabs_b, seq, tm)
        gates = _gates(h, w_g[l], min(TM_GATES, 2 * tm), dm)
        oa = _attention(q_a, k_a, vt_a, gates, batch=batch, seq=seq, kv_heads=A_KV_HEADS,
                        group=A_GROUP, dk=A_HEAD_DIM, dv=A_HEAD_DIM, tq=tq_a, tk=tk,
                        n_sub=n_sub_a, hps=1, gate_col0=0)
        ob = _attention(q_b, k_b, vt_b, gates, batch=batch, seq=seq, kv_heads=B_HEADS,
                        group=1, dk=B_QK_PAD, dv=B_V, tq=tq_b, tk=tk, n_sub=n_sub_b,
                        hps=hps_b, gate_col0=A_WIDTH)
        last = l == depth - 1
        g_next = final_g if last else norm_g[l + 1]
        res = _merge_out(oa, ob, gates, xt, w_branch[l, 0], w_branch[l, 1], w_out[l],
                         g_next[None, :], min(TM_OUT, tm), last)
        if last:
            xt = res[0]
        else:
            xt, h = res
    return xt.reshape(batch, seq, dm)


def kernel(x_prompt, x_sample, norm_g, w_in, q_norm_g, k_norm_g, kv_norm_g, w_ukv, w_branch,
           w_out, final_norm_g):
    w_a, w_b, w_g = _split_w_in(w_in)
    weights = (norm_g, w_a, w_b, w_g, q_norm_g, k_norm_g, kv_norm_g, w_ukv.astype(BF16),
               w_branch.astype(BF16), w_out.astype(BF16))
    y_prompt = _trunk(x_prompt, weights, final_norm_g)
    y_sample = _trunk(x_sample, weights, final_norm_g)
    return (y_prompt, y_sample)
```
